```python
import math
import jax, jax.numpy as jnp
from jax import lax
import numpy as np

D_MODEL = 2048
BATCH = 4
SEQ = 4096
DEPTH = 4

N_MIXERS = 2
EPS = 1e-6
DA_HEADS = 8
DA_HEAD_DIM = 128
DA_V_DIM = 2 * DA_HEAD_DIM
DA_QK_WIDTH = DA_HEADS * 2 * DA_HEAD_DIM
DA_V_WIDTH = DA_HEADS * DA_V_DIM
ROPE_THETA = 500000.0
ROPE_DIM = DA_HEAD_DIM // 4
Q_BLOCK = 128
SSM_D_INNER = 2 * D_MODEL
SSM_HEAD_DIM = 64
SSM_HEADS = SSM_D_INNER // SSM_HEAD_DIM
SSM_GROUPS = 8
SSM_HPG = SSM_HEADS // SSM_GROUPS
SSM_D_STATE = 128
SSM_CONV = 4
SSM_CHUNK = 128
SSM_CONV_DIM = SSM_D_INNER + 2 * SSM_GROUPS * SSM_D_STATE
SSM_IN_DIM = SSM_D_INNER + SSM_CONV_DIM + SSM_HEADS
D_FF = 4 * D_MODEL
N_ATTN_LAYERS = (DEPTH + 1) // 2
N_SSM_LAYERS = DEPTH // 2

kernel_name = "hybrid_diffattn_mamba2_sqrelu"


def rms_norm(x, gain):
    xf = x.astype(jnp.float32)
    y = xf * lax.rsqrt(jnp.mean(xf * xf, axis=-1, keepdims=True) + EPS)
    return (y * gain.astype(jnp.float32)).astype(x.dtype)


def lambda_init(layer_idx):
    return 0.8 - 0.6 * math.exp(-0.3 * layer_idx)


def rope_tables(seq):
    inv = ROPE_THETA ** (-jnp.arange(0, ROPE_DIM, 2, dtype=jnp.float32) / ROPE_DIM)
    ang = jnp.arange(seq, dtype=jnp.float32)[:, None] * inv[None, :]
    return jnp.cos(ang)[None, :, None, None, :], jnp.sin(ang)[None, :, None, None, :]


def partial_rope(x, cos, sin):
    half = ROPE_DIM // 2
    cos = cos.astype(x.dtype)
    sin = sin.astype(x.dtype)
    x1 = x[..., :half]
    x2 = x[..., half:ROPE_DIM]
    return jnp.concatenate([x1 * cos - x2 * sin, x2 * cos + x1 * sin, x[..., ROPE_DIM:]], axis=-1)


def diff_attention(h, w_qkv, q_gain, k_gain, lam, subln_gain, w_o, lam_init):
    b, s, _ = h.shape
    qkv = h @ w_qkv
    q, k, v = jnp.split(qkv, [DA_QK_WIDTH, 2 * DA_QK_WIDTH], axis=-1)
    q = q.reshape(b, s, DA_HEADS, 2, DA_HEAD_DIM)
    k = k.reshape(b, s, DA_HEADS, 2, DA_HEAD_DIM)
    v = v.reshape(b, s, DA_HEADS, DA_V_DIM)
    cos, sin = rope_tables(s)
    q = partial_rope(rms_norm(q, q_gain), cos, sin)
    k = partial_rope(rms_norm(k, k_gain), cos, sin)
    lamf = lam.astype(jnp.float32)
    lam_full = jnp.exp(jnp.sum(lamf[0] * lamf[1])) - jnp.exp(jnp.sum(lamf[2] * lamf[3])) + lam_init
    scale = DA_HEAD_DIM ** -0.5
    nblk = s // Q_BLOCK
    qb = (q * scale).reshape(b, nblk, Q_BLOCK, DA_HEADS, 2, DA_HEAD_DIM).transpose(1, 0, 3, 4, 2, 5)
    kt = k.transpose(0, 2, 3, 1, 4)
    vt = v.transpose(0, 2, 1, 3)
    key_pos = jnp.arange(s)

    def block(args):
        qblk, i = args
        scores = jnp.einsum('bhcqd,bhckd->bhcqk', qblk, kt, preferred_element_type=jnp.float32)
        q_pos = i * Q_BLOCK + jnp.arange(Q_BLOCK)
        mask = key_pos[None, :] <= q_pos[:, None]
        p = jax.nn.softmax(jnp.where(mask, scores, -jnp.inf), axis=-1)
        a = p[:, :, 0] - lam_full * p[:, :, 1]
        return jnp.einsum('bhqk,bhkv->bhqv', a.astype(vt.dtype), vt)

    o = lax.map(block, (qb, jnp.arange(nblk)))
    o = o.transpose(1, 0, 3, 2, 4).reshape(b, s, DA_HEADS, DA_V_DIM)
    o = rms_norm(o, subln_gain) * (1.0 - lam_init)
    return o.reshape(b, s, DA_V_WIDTH) @ w_o


def ssd_chunked(x, dt, A, B, C):
    b, s = x.shape[:2]
    nc = s // SSM_CHUNK
    L = SSM_CHUNK

    def to_chunks(t):
        return jnp.moveaxis(t.reshape(b, nc, L, *t.shape[2:]), 1, 0)

    xc, dtc, Bc, Cc = to_chunks(x), to_chunks(dt), to_chunks(B), to_chunks(C)
    causal = jnp.tril(jnp.ones((L, L), dtype=bool))

    def step(state, inp):
        x_, dt_, B_, C_ = inp
        a = jnp.moveaxis(jnp.cumsum(dt_ * A, axis=1), 1, -1)
        xdt = x_ * dt_[..., None]
        seg = a[..., :, None] - a[..., None, :]
        decay = jnp.exp(jnp.where(causal, seg, -jnp.inf))
        cb = jnp.einsum('blgn,bsgn->bgls', C_, B_)
        y_diag = jnp.einsum('bgls,bghls,bsghp->blghp', cb, decay, xdt)
        y_off = jnp.einsum('blgn,bghpn,bghl->blghp', C_, state, jnp.exp(a))
        decay_to_end = jnp.exp(a[..., -1:] - a)
        new_state = state * jnp.exp(a[..., -1])[..., None, None] + jnp.einsum('bsgn,bghs,bsghp->bghpn', B_, decay_to_end, xdt)
        return new_state, y_diag + y_off

    state0 = jnp.zeros((b, SSM_GROUPS, SSM_HPG, SSM_HEAD_DIM, SSM_D_STATE), jnp.float32)
    _, y = lax.scan(step, state0, (xc, dtc, Bc, Cc))
    return jnp.moveaxis(y, 0, 1).reshape(b, s, SSM_GROUPS, SSM_HPG, SSM_HEAD_DIM)


def mamba2_mixer(h, w_in, conv_w, conv_b, dt_bias, a_log, d_skip, norm_gain, w_out):
    b, s, _ = h.shape
    zxbcdt = h @ w_in
    z, xbc, dt = jnp.split(zxbcdt, [SSM_D_INNER, SSM_D_INNER + SSM_CONV_DIM], axis=-1)
    xbc = lax.conv_general_dilated(xbc, conv_w[:, None, :].astype(xbc.dtype), window_strides=(1,),
                                   padding=[(SSM_CONV - 1, 0)], dimension_numbers=('NWC', 'WIO', 'NWC'),
                                   feature_group_count=SSM_CONV_DIM) + conv_b
    xbc = jax.nn.silu(xbc).astype(jnp.float32)
    xs, Bm, Cm = jnp.split(xbc, [SSM_D_INNER, SSM_D_INNER + SSM_GROUPS * SSM_D_STATE], axis=-1)
    dt = jax.nn.softplus(dt.astype(jnp.float32) + dt_bias.astype(jnp.float32))
    A = -jnp.exp(a_log.astype(jnp.float32))
    xs = xs.reshape(b, s, SSM_GROUPS, SSM_HPG, SSM_HEAD_DIM)
    y = ssd_chunked(xs, dt.reshape(b, s, SSM_GROUPS, SSM_HPG), A.reshape(SSM_GROUPS, SSM_HPG),
                    Bm.reshape(b, s, SSM_GROUPS, SSM_D_STATE), Cm.reshape(b, s, SSM_GROUPS, SSM_D_STATE))
    y = y + d_skip.astype(jnp.float32).reshape(SSM_GROUPS, SSM_HPG)[:, :, None] * xs
    y = y.reshape(b, s, SSM_D_INNER) * jax.nn.silu(z.astype(jnp.float32))
    y = rms_norm(y.reshape(b, s, SSM_GROUPS, SSM_D_INNER // SSM_GROUPS),
                 norm_gain.reshape(SSM_GROUPS, SSM_D_INNER // SSM_GROUPS)).reshape(b, s, SSM_D_INNER)
    return y.astype(h.dtype) @ w_out


def sq_relu_mlp(h, w1, w2):
    return jnp.square(jax.nn.relu(h @ w1)) @ w2


def setup_inputs(seed: int = 0) -> dict:
    key = jax.random.key(seed)
    ks = jax.random.split(key, 20)
    f32 = jnp.float32
    nrm = lambda k, shape, scale: jax.random.normal(k, shape, f32) * scale
    dt0 = jnp.exp(jax.random.uniform(ks[11], (N_SSM_LAYERS, SSM_HEADS), f32, math.log(1e-3), math.log(1e-1)))
    return {
        "x": nrm(ks[0], (BATCH, SEQ, D_MODEL), 1.0),
        "mixer_norm": 1.0 + nrm(ks[1], (DEPTH, D_MODEL), 0.02),
        "mlp_norm": 1.0 + nrm(ks[2], (DEPTH, D_MODEL), 0.02),
        "attn_w_qkv": nrm(ks[3], (N_ATTN_LAYERS, D_MODEL, 2 * DA_QK_WIDTH + DA_V_WIDTH), D_MODEL ** -0.5),
        "attn_q_norm": 1.0 + nrm(ks[4], (N_ATTN_LAYERS, DA_HEAD_DIM), 0.02),
        "attn_k_norm": 1.0 + nrm(ks[5], (N_ATTN_LAYERS, DA_HEAD_DIM), 0.02),
        "attn_lambda": nrm(ks[6], (N_ATTN_LAYERS, 4, DA_HEAD_DIM), 0.1),
        "attn_subln": 1.0 + nrm(ks[7], (N_ATTN_LAYERS, DA_V_DIM), 0.02),
        "attn_w_o": nrm(ks[8], (N_ATTN_LAYERS, DA_V_WIDTH, D_MODEL), DA_V_WIDTH ** -0.5),
        "ssm_w_in": nrm(ks[9], (N_SSM_LAYERS, D_MODEL, SSM_IN_DIM), D_MODEL ** -0.5),
        "ssm_conv_w": nrm(ks[10], (N_SSM_LAYERS, SSM_CONV, SSM_CONV_DIM), SSM_CONV ** -0.5),
        "ssm_conv_b": nrm(ks[12], (N_SSM_LAYERS, SSM_CONV_DIM), 0.02),
        "ssm_dt_bias": dt0 + jnp.log(-jnp.expm1(-dt0)),
        "ssm_a_log": jnp.log(jax.random.uniform(ks[13], (N_SSM_LAYERS, SSM_HEADS), f32, 1.0, 16.0)),
        "ssm_d": 1.0 + nrm(ks[14], (N_SSM_LAYERS, SSM_HEADS), 0.02),
        "ssm_norm": 1.0 + nrm(ks[15], (N_SSM_LAYERS, SSM_D_INNER), 0.02),
        "ssm_w_out": nrm(ks[16], (N_SSM_LAYERS, SSM_D_INNER, D_MODEL), SSM_D_INNER ** -0.5),
        "mlp_w1": nrm(ks[17], (DEPTH, D_MODEL, D_FF), D_MODEL ** -0.5),
        "mlp_w2": nrm(ks[18], (DEPTH, D_FF, D_MODEL), D_FF ** -0.5),
    }


def reference(x, mixer_norm, mlp_norm, attn_w_qkv, attn_q_norm, attn_k_norm, attn_lambda, attn_subln, attn_w_o,
              ssm_w_in, ssm_conv_w, ssm_conv_b, ssm_dt_bias, ssm_a_log, ssm_d, ssm_norm, ssm_w_out,
              mlp_w1, mlp_w2):
    h = x
    for i in range(DEPTH):
        hn = rms_norm(h, mixer_norm[i])
        j = i // N_MIXERS
        if i % N_MIXERS == 0:
            mix = diff_attention(hn, attn_w_qkv[j], attn_q_norm[j], attn_k_norm[j], attn_lambda[j],
                                 attn_subln[j], attn_w_o[j], lambda_init(i))
        else:
            mix = mamba2_mixer(hn, ssm_w_in[j], ssm_conv_w[j], ssm_conv_b[j], ssm_dt_bias[j], ssm_a_log[j],
                               ssm_d[j], ssm_norm[j], ssm_w_out[j])
        h = h + mix.astype(h.dtype)
        h = h + sq_relu_mlp(rms_norm(h, mlp_norm[i]), mlp_w1[i], mlp_w2[i]).astype(h.dtype)
    return h
```

```python
import functools
import math

import jax
import jax.numpy as jnp
from jax import lax
from jax.experimental import pallas as pl
from jax.experimental.pallas import tpu as pltpu

F32 = jnp.float32
BF16 = jnp.bfloat16

EPS = 1e-6
LANES = 128
SUBLANES = 8
VMEM_LIMIT_BYTES = 56 * 1024 * 1024

DA_HEADS = 8
DA_HEAD_DIM = 128
DA_V_DIM = 2 * DA_HEAD_DIM
DA_QK_WIDTH = DA_HEADS * 2 * DA_HEAD_DIM
DA_V_WIDTH = DA_HEADS * DA_V_DIM
ROPE_THETA = 500000.0
ROPE_DIM = DA_HEAD_DIM // 4
SSM_HEAD_DIM = 64
SSM_GROUPS = 8
SSM_HPG = 8
SSM_D_STATE = 128
SSM_CONV = 4
SSM_CHUNK = 128
SSM_GROUP_WIDTH = SSM_HPG * SSM_HEAD_DIM
SSM_D_INNER = SSM_GROUPS * SSM_GROUP_WIDTH
SSM_HEADS = SSM_GROUPS * SSM_HPG
SSM_BC_WIDTH = SSM_GROUPS * SSM_D_STATE
SSM_ZX_WIDTH = 2 * SSM_D_INNER + 2 * SSM_BC_WIDTH

ACT_DTYPE = BF16


def _lambda_init(layer_idx):
    return 0.8 - 0.6 * math.exp(-0.3 * layer_idx)


def _sigmoid(v):
    return 1.0 / (1.0 + jnp.exp(-v))


def _softplus(v):
    return jnp.maximum(v, 0.0) + jnp.log(1.0 + jnp.exp(-jnp.abs(v)))


def _params(sem):
    return pltpu.CompilerParams(dimension_semantics=sem, vmem_limit_bytes=VMEM_LIMIT_BYTES)


def _pick(total, pref):
    t = min(total, pref)
    assert total % t == 0, (total, pref)
    return t


def _rope_group(v, gain, cosf, sina, sinb):
    ms = jnp.mean(v * v, axis=-1, keepdims=True)
    y = v * lax.rsqrt(ms + EPS) * gain
    up = pltpu.roll(y, LANES - ROPE_DIM // 2, axis=1)
    dn = pltpu.roll(y, ROPE_DIM // 2, axis=1)
    return y * cosf + up * sina + dn * sinb


def _norm_mm_kernel(x_ref, g_ref, w_ref, *rest, mode, tn, q_scale):
    if mode == "qkv":
        qg_ref, kg_ref, cos_ref, sa_ref, sb_ref, o_ref, xn_ref = rest
    else:
        o_ref, xn_ref = rest
    j = pl.program_id(1)

    @pl.when(j == 0)
    def _():
        x = x_ref[...]
        ms = jnp.mean(x * x, axis=-1, keepdims=True)
        xn_ref[...] = (x * lax.rsqrt(ms + EPS) * g_ref[...]).astype(BF16)

    acc = jnp.dot(xn_ref[...], w_ref[...], preferred_element_type=F32)
    if mode == "plain":
        o_ref[...] = acc.astype(o_ref.dtype)
    elif mode == "relu2":
        r = jnp.maximum(acc, 0.0)
        o_ref[...] = (r * r).astype(o_ref.dtype)
    else:
        n_q = DA_QK_WIDTH // tn

        @pl.when(j < 2 * n_q)
        def _():
            gain = jnp.where(j < n_q, qg_ref[...] * q_scale, kg_ref[...])
            cosf, sina, sinb = cos_ref[...], sa_ref[...], sb_ref[...]
            for g in range(tn // LANES):
                sl = slice(g * LANES, (g + 1) * LANES)
                o_ref[:, sl] = _rope_group(acc[:, sl], gain, cosf, sina, sinb).astype(o_ref.dtype)

        @pl.when(j >= 2 * n_q)
        def _():
            o_ref[...] = acc.astype(o_ref.dtype)


def _norm_matmul(x, gain, w, *, mode, out_dtype, tm_pref=1024, tn_pref=1024, qkv_extra=None, seq=None):
    t, k = x.shape
    n = w.shape[1]
    tm = _pick(seq if mode == "qkv" else t, tm_pref)
    tn = _pick(n, tn_pref)
    grid = (t // tm, n // tn)
    in_specs = [
        pl.BlockSpec((tm, k), lambda i, j: (i, 0)),
        pl.BlockSpec((1, k), lambda i, j: (0, 0)),
        pl.BlockSpec((k, tn), lambda i, j: (0, j)),
    ]
    args = [x, gain.reshape(1, k), w]
    if mode == "qkv":
        qg, kg, cosf, sina, sinb = qkv_extra
        assert seq % tm == 0 and DA_QK_WIDTH % tn == 0
        n_s = seq // tm
        vec = pl.BlockSpec((1, LANES), lambda i, j: (0, 0))
        tab = pl.BlockSpec((tm, LANES), lambda i, j: (i % n_s, 0))
        in_specs += [vec, vec, tab, tab, tab]
        args += [qg.reshape(1, LANES), kg.reshape(1, LANES), cosf, sina, sinb]
    return pl.pallas_call(
        functools.partial(_norm_mm_kernel, mode=mode, tn=tn, q_scale=DA_HEAD_DIM ** -0.5),
        grid=grid,
        in_specs=in_specs,
        out_specs=pl.BlockSpec((tm, tn), lambda i, j: (i, j)),
        out_shape=jax.ShapeDtypeStruct((t, n), out_dtype),
        scratch_shapes=[pltpu.VMEM((tm, k), BF16)],
        compiler_params=_params(("parallel", "arbitrary")),
        name="norm_mm_" + mode,
    )(*args)


def _mm_res_kernel(x_ref, w_ref, r_ref, o_ref, *scratch, nk):
    p = jnp.dot(x_ref[...], w_ref[...], preferred_element_type=F32)
    if nk == 1:
        o_ref[...] = r_ref[...] + p
        return
    acc_ref, = scratch
    k = pl.program_id(2)

    @pl.when(k == 0)
    def _():
        acc_ref[...] = p

    if nk > 2:
        @pl.when((k > 0) & (k < nk - 1))
        def _():
            acc_ref[...] += p

    @pl.when(k == nk - 1)
    def _():
        o_ref[...] = r_ref[...] + (acc_ref[...] + p)


def _matmul_residual(x, w, res, *, tm_pref=1024, tn_pref=1024, tk_pref=2048):
    t, k = x.shape
    n = w.shape[1]
    tm, tn, tk = _pick(t, tm_pref), _pick(n, tn_pref), _pick(k, tk_pref)
    nk = k // tk
    return pl.pallas_call(
        functools.partial(_mm_res_kernel, nk=nk),
        grid=(t // tm, n // tn, nk),
        in_specs=[
            pl.BlockSpec((tm, tk), lambda i, j, kk: (i, kk)),
            pl.BlockSpec((tk, tn), lambda i, j, kk: (kk, j)),
            pl.BlockSpec((tm, tn), lambda i, j, kk: (i, j)),
        ],
        out_specs=pl.BlockSpec((tm, tn), lambda i, j, kk: (i, j)),
        out_shape=jax.ShapeDtypeStruct((t, n), F32),
        scratch_shapes=[pltpu.VMEM((tm, tn), F32)] if nk > 1 else [],
        compiler_params=_params(("parallel", "parallel", "arbitrary")),
        name="mm_res",
    )(x, w, res)


NEG_BIG = -1e30


def _attn_kernel(lam_ref, sg_ref, q_ref, k_ref, v_ref, o_ref, m_ref, l_ref, acc_ref, *, tq, lam_init):
    qi = pl.program_id(2)
    d = DA_HEAD_DIM
    q = q_ref[0]
    qs = (q[:, :d], q[:, d:])

    m_ref[...] = jnp.full(m_ref.shape, NEG_BIG, F32)
    l_ref[...] = jnp.zeros(l_ref.shape, F32)
    acc_ref[...] = jnp.zeros(acc_ref.shape, F32)

    def step(j, masked):
        start = pl.multiple_of(j * tq, tq)
        kb = k_ref[0, pl.ds(start, tq), :]
        vb = v_ref[0, pl.ds(start, tq), :]
        for c in range(2):
            s = lax.dot_general(qs[c], kb[:, c * d:(c + 1) * d], (((1,), (1,)), ((), ())),
                                preferred_element_type=F32)
            if masked:
                row = lax.broadcasted_iota(jnp.int32, (tq, tq), 0)
                col = lax.broadcasted_iota(jnp.int32, (tq, tq), 1)
                s = jnp.where(col <= row, s, NEG_BIG)
            m_old = m_ref[c]
            m_new = jnp.maximum(m_old, jnp.max(s, axis=-1, keepdims=True))
            alpha = jnp.exp(m_old - m_new)
            p = jnp.exp(s - m_new)
            l_ref[c] = alpha * l_ref[c] + jnp.sum(p, axis=-1, keepdims=True)
            acc_ref[c] = alpha * acc_ref[c] + jnp.dot(p.astype(BF16), vb, preferred_element_type=F32)
            m_ref[c] = m_new

    def body(j, carry):
        step(j, False)
        return carry

    lax.fori_loop(0, qi, body, 0)
    step(qi, True)

    lam = lam_ref[...]
    lam_full = (jnp.exp(jnp.sum(lam[0:1] * lam[1:2], axis=-1, keepdims=True))
                - jnp.exp(jnp.sum(lam[2:3] * lam[3:4], axis=-1, keepdims=True)) + lam_init)
    o = acc_ref[0] / l_ref[0] - lam_full * (acc_ref[1] / l_ref[1])
    ms = jnp.mean(o * o, axis=-1, keepdims=True)
    o_ref[0] = (o * lax.rsqrt(ms + EPS) * (sg_ref[...] * (1.0 - lam_init))).astype(o_ref.dtype)


def _diff_attention(qkv, lam, subln, *, batch, seq, lam_init, tq_pref=512):
    tq = _pick(seq, tq_pref)
    hw = 2 * DA_HEAD_DIM
    kern = functools.partial(_attn_kernel, tq=tq, lam_init=lam_init)
    return pl.pallas_call(
        kern,
        grid=(batch, DA_HEADS, seq // tq),
        in_specs=[
            pl.BlockSpec((4, DA_HEAD_DIM), lambda b, h, i: (0, 0)),
            pl.BlockSpec((1, DA_V_DIM), lambda b, h, i: (0, 0)),
            pl.BlockSpec((1, tq, hw), lambda b, h, i: (b, i, h)),
            pl.BlockSpec((1, seq, hw), lambda b, h, i: (b, 0, DA_HEADS + h)),
            pl.BlockSpec((1, seq, DA_V_DIM), lambda b, h, i: (b, 0, 2 * DA_HEADS + h)),
        ],
        out_specs=pl.BlockSpec((1, tq, DA_V_DIM), lambda b, h, i: (b, i, h)),
        out_shape=jax.ShapeDtypeStruct((batch, seq, DA_V_WIDTH), ACT_DTYPE),
        scratch_shapes=[
            pltpu.VMEM((2, tq, 1), F32),
            pltpu.VMEM((2, tq, 1), F32),
            pltpu.VMEM((2, tq, DA_V_DIM), F32),
        ],
        compiler_params=_params(("parallel", "parallel", "arbitrary")),
        name="diff_attn",
    )(lam, subln.reshape(1, DA_V_DIM), qkv, qkv, qkv)


def _split3(v):
    hi = v.astype(BF16)
    r1 = v - hi.astype(F32)
    mid = r1.astype(BF16)
    lo = (r1 - mid.astype(F32)).astype(BF16)
    return hi, mid, lo


def _ssd_prep_kernel(dt_ref, bias_ref, alog_ref, a_ref, at_ref, dtt_ref, *, ts):
    L = SSM_CHUNK
    dt = _softplus(dt_ref[0] + bias_ref[...])
    dta = dt * (-jnp.exp(alog_ref[...]))
    row = lax.broadcasted_iota(jnp.int32, (L, L), 0)
    col = lax.broadcasted_iota(jnp.int32, (L, L), 1)
    tril = jnp.where(row >= col, 1.0, 0.0).astype(BF16)
    for ci in range(ts // L):
        sl = slice(ci * L, (ci + 1) * L)
        hi, mid, lo = _split3(dta[sl])
        a = (jnp.dot(tril, lo, preferred_element_type=F32) + jnp.dot(tril, mid, preferred_element_type=F32)
             + jnp.dot(tril, hi, preferred_element_type=F32))
        a_ref[0, sl, :] = a
        at_ref[0, :, sl] = a.T
        dtt_ref[0, :, sl] = dt[sl].T


def _ssd_prep(dt_raw, dt_bias, a_log, *, batch, seq, ts_pref=512):
    ts = _pick(seq, ts_pref)
    pad = LANES - SSM_HEADS
    bias = jnp.pad(dt_bias.astype(F32), (0, pad)).reshape(1, LANES)
    alog = jnp.pad(a_log.astype(F32), (0, pad)).reshape(1, LANES)
    col = pl.BlockSpec((1, ts, LANES), lambda b, c: (b, c, 0))
    rowl = pl.BlockSpec((1, LANES, ts), lambda b, c: (b, 0, c))
    vec = pl.BlockSpec((1, LANES), lambda b, c: (0, 0))
    return pl.pallas_call(
        functools.partial(_ssd_prep_kernel, ts=ts),
        grid=(batch, seq // ts),
        in_specs=[col, vec, vec],
        out_specs=[col, rowl, rowl],
        out_shape=[jax.ShapeDtypeStruct((batch, seq, LANES), F32),
                   jax.ShapeDtypeStruct((batch, LANES, seq), F32),
                   jax.ShapeDtypeStruct((batch, LANES, seq), F32)],
        compiler_params=_params(("parallel", "parallel")),
        name="ssd_prep",
    )(dt_raw.reshape(batch, seq, LANES), bias, alog)


def _ssd_kernel(x_ref, z_ref, b_ref, c_ref, cwx_ref, cwb_ref, cwc_ref, cbx_ref, cbb_ref, cbc_ref,
                ag_ref, at_ref, dtt_ref, d_ref, ng_ref, o_ref, xp_ref, bp_ref, cp_ref, st_ref, *, ts):
    L = SSM_CHUNK
    tail = SUBLANES
    cidx = pl.program_id(2)

    @pl.when(cidx == 0)
    def _():
        xp_ref[0:tail, :] = jnp.zeros((tail, xp_ref.shape[1]), F32)
        bp_ref[0:tail, :] = jnp.zeros((tail, bp_ref.shape[1]), F32)
        cp_ref[0:tail, :] = jnp.zeros((tail, cp_ref.shape[1]), F32)
        st_ref[...] = jnp.zeros(st_ref.shape, F32)

    xp_ref[tail:tail + ts, :] = x_ref[0].astype(F32)
    bp_ref[tail:tail + ts, :] = b_ref[0].astype(F32)
    cp_ref[tail:tail + ts, :] = c_ref[0].astype(F32)

    row = lax.broadcasted_iota(jnp.int32, (L, L), 0)
    col = lax.broadcasted_iota(jnp.int32, (L, L), 1)
    causal = row >= col
    low_half = lax.broadcasted_iota(jnp.int32, (1, LANES), 1) < SSM_HEAD_DIM

    def conv_silu(pad_ref, w_ref, bias_ref, r0):
        acc = bias_ref[...]
        for k in range(SSM_CONV):
            off = r0 + tail - (SSM_CONV - 1) + k
            acc = acc + w_ref[k:k + 1, :] * pad_ref[off:off + L, :]
        return acc * _sigmoid(acc)

    for ci in range(ts // L):
        r0 = ci * L
        xc = conv_silu(xp_ref, cwx_ref, cbx_ref, r0)
        bc = conv_silu(bp_ref, cwb_ref, cbb_ref, r0)
        cc = conv_silu(cp_ref, cwc_ref, cbc_ref, r0)
        cb = lax.dot_general(cc.astype(BF16), bc.astype(BF16), (((1,), (1,)), ((), ())),
                             preferred_element_type=F32)
        bt = bc.T
        acol = ag_ref[0, 0, r0:r0 + L, :]
        arow = at_ref[0, :, r0:r0 + L]
        dtrow = dtt_ref[0, :, r0:r0 + L]
        wrow = dtrow * jnp.exp(arow[:, L - 1:L] - arow)
        st = st_ref[...]
        xcb = xc.astype(BF16)
        stb = st.astype(BF16)
        ys, news = [], []
        for j in range(SSM_HPG // 2):
            sl = slice(j * LANES, (j + 1) * LANES)
            xpair = xcb[:, sl]
            rhs = jnp.concatenate([xpair, stb[:, sl]], axis=0)
            y2, s2, d2 = [], [], []
            for hh in range(2):
                h = 2 * j + hh
                ac = acol[:, h:h + 1]
                seg = ac - arow[h:h + 1, :]
                dec = jnp.exp(jnp.where(causal, seg, -jnp.inf))
                m = (cb * dec * dtrow[h:h + 1, :]).astype(BF16)
                cs = (cc * jnp.exp(ac)).astype(BF16)
                lhs = jnp.concatenate([m, cs], axis=1)
                y2.append(jnp.dot(lhs, rhs, preferred_element_type=F32))
                bw = (bt * wrow[h:h + 1, :]).astype(BF16)
                s2.append(jnp.dot(bw, xpair, preferred_element_type=F32))
                d2.append(jnp.exp(acol[L - 1:L, h:h + 1]))
            ys.append(jnp.where(low_half, y2[0], y2[1]))
            news.append(st[:, sl] * jnp.where(low_half, d2[0], d2[1]) + jnp.where(low_half, s2[0], s2[1]))
        st_ref[...] = jnp.concatenate(news, axis=1)
        y = jnp.concatenate(ys, axis=1) + d_ref[...] * xc
        zz = z_ref[0, r0:r0 + L, :].astype(F32)
        y = y * (zz * _sigmoid(zz))
        ms = jnp.mean(y * y, axis=-1, keepdims=True)
        o_ref[0, r0:r0 + L, :] = (y * lax.rsqrt(ms + EPS) * ng_ref[...]).astype(o_ref.dtype)

    xp_ref[0:tail, :] = xp_ref[ts:ts + tail, :]
    bp_ref[0:tail, :] = bp_ref[ts:ts + tail, :]
    cp_ref[0:tail, :] = cp_ref[ts:ts + tail, :]


def _ssd(zx, a_g, a_t, dt_t, conv_w, conv_b, d_skip, norm_gain, *, batch, seq, ts_pref=256):
    ts = _pick(seq, ts_pref)
    gw, n, G = SSM_GROUP_WIDTH, SSM_D_STATE, SSM_GROUPS
    x_blk0 = SSM_D_INNER // gw
    b_blk0 = 2 * SSM_D_INNER // n
    c_blk0 = b_blk0 + SSM_BC_WIDTH // n
    cw = conv_w.astype(F32)
    cbias = conv_b.astype(F32).reshape(1, -1)
    d_exp = jnp.repeat(d_skip.astype(F32), SSM_HEAD_DIM).reshape(1, SSM_D_INNER)
    ng = norm_gain.astype(F32).reshape(1, SSM_D_INNER)
    cwb0 = SSM_D_INNER // n
    cwc0 = cwb0 + SSM_BC_WIDTH // n
    in_specs = [
        pl.BlockSpec((1, ts, gw), lambda b, g, c: (b, c, x_blk0 + g)),
        pl.BlockSpec((1, ts, gw), lambda b, g, c: (b, c, g)),
        pl.BlockSpec((1, ts, n), lambda b, g, c: (b, c, b_blk0 + g)),
        pl.BlockSpec((1, ts, n), lambda b, g, c: (b, c, c_blk0 + g)),
        pl.BlockSpec((SSM_CONV, gw), lambda b, g, c: (0, g)),
        pl.BlockSpec((SSM_CONV, n), lambda b, g, c: (0, cwb0 + g)),
        pl.BlockSpec((SSM_CONV, n), lambda b, g, c: (0, cwc0 + g)),
        pl.BlockSpec((1, gw), lambda b, g, c: (0, g)),
        pl.BlockSpec((1, n), lambda b, g, c: (0, cwb0 + g)),
        pl.BlockSpec((1, n), lambda b, g, c: (0, cwc0 + g)),
        pl.BlockSpec((1, 1, ts, SSM_HPG), lambda b, g, c: (b, g, c, 0)),
        pl.BlockSpec((1, SSM_HPG, ts), lambda b, g, c: (b, g, c)),
        pl.BlockSpec((1, SSM_HPG, ts), lambda b, g, c: (b, g, c)),
        pl.BlockSpec((1, gw), lambda b, g, c: (0, g)),
        pl.BlockSpec((1, gw), lambda b, g, c: (0, g)),
    ]
    return pl.pallas_call(
        functools.partial(_ssd_kernel, ts=ts),
        grid=(batch, G, seq // ts),
        in_specs=in_specs,
        out_specs=pl.BlockSpec((1, ts, gw), lambda b, g, c: (b, c, g)),
        out_shape=jax.ShapeDtypeStruct((batch, seq, SSM_D_INNER), ACT_DTYPE),
        scratch_shapes=[
            pltpu.VMEM((ts + SUBLANES, gw), F32),
            pltpu.VMEM((ts + SUBLANES, n), F32),
            pltpu.VMEM((ts + SUBLANES, n), F32),
            pltpu.VMEM((n, gw), F32),
        ],
        compiler_params=_params(("parallel", "parallel", "arbitrary")),
        name="ssd_scan",
    )(zx, zx, zx, zx, cw, cw, cw, cbias, cbias, cbias, a_g, a_t, dt_t, d_exp, ng)


def _rope_tables(seq):
    half = ROPE_DIM // 2
    inv = ROPE_THETA ** (-jnp.arange(0, ROPE_DIM, 2, dtype=F32) / ROPE_DIM)
    ang = jnp.arange(seq, dtype=F32)[:, None] * inv[None, :]
    cos, sin = jnp.cos(ang), jnp.sin(ang)
    zeros = jnp.zeros((seq, LANES - half), F32)
    cosf = jnp.concatenate([cos, cos, jnp.ones((seq, LANES - ROPE_DIM), F32)], axis=1)
    sina = jnp.concatenate([-sin, zeros], axis=1)
    sinb = jnp.concatenate([jnp.zeros((seq, half), F32), sin, zeros[:, half:]], axis=1)
    return cosf, sina, sinb


def _attn_layer(h, norm_g, w_qkv, q_gain, k_gain, lam, subln, w_o, lam_init, tables, *, batch, seq):
    qkv = _norm_matmul(h, norm_g, w_qkv.astype(BF16), mode="qkv", out_dtype=ACT_DTYPE,
                       qkv_extra=(q_gain.astype(F32), k_gain.astype(F32)) + tables, seq=seq)
    o = _diff_attention(qkv.reshape(batch, seq, -1), lam.astype(F32), subln.astype(F32),
                        batch=batch, seq=seq, lam_init=lam_init)
    return _matmul_residual(o.reshape(batch * seq, DA_V_WIDTH), w_o.astype(BF16), h)


def _ssm_layer(h, norm_g, w_in, conv_w, conv_b, dt_bias, a_log, d_skip, norm_gain, w_out, *, batch, seq):
    w_zx = w_in[:, :SSM_ZX_WIDTH].astype(BF16)
    w_dt = jnp.pad(w_in[:, SSM_ZX_WIDTH:], ((0, 0), (0, LANES - SSM_HEADS))).astype(BF16)
    zx = _norm_matmul(h, norm_g, w_zx, mode="plain", out_dtype=ACT_DTYPE)
    dt_raw = _norm_matmul(h, norm_g, w_dt, mode="plain", out_dtype=F32)
    a_c, a_t, dt_t = _ssd_prep(dt_raw, dt_bias, a_log, batch=batch, seq=seq)
    a_g = a_c[:, :, :SSM_HEADS].reshape(batch, seq, SSM_GROUPS, SSM_HPG).transpose(0, 2, 1, 3)
    y = _ssd(zx.reshape(batch, seq, SSM_ZX_WIDTH), a_g, a_t, dt_t, conv_w, conv_b, d_skip, norm_gain,
             batch=batch, seq=seq)
    return _matmul_residual(y.reshape(batch * seq, SSM_D_INNER), w_out.astype(BF16), h)


def _mlp_layer(h, norm_g, w1, w2):
    a = _norm_matmul(h, norm_g, w1.astype(BF16), mode="relu2", out_dtype=ACT_DTYPE)
    return _matmul_residual(a, w2.astype(BF16), h)


def kernel(x, mixer_norm, mlp_norm, attn_w_qkv, attn_q_norm, attn_k_norm, attn_lambda, attn_subln, attn_w_o, ssm_w_in, ssm_conv_w, ssm_conv_b, ssm_dt_bias, ssm_a_log, ssm_d, ssm_norm, ssm_w_out, mlp_w1, mlp_w2):
    batch, seq, d_model = x.shape
    depth = mixer_norm.shape[0]
    h = x.reshape(batch * seq, d_model).astype(F32)
    tables = _rope_tables(seq)
    for i in range(depth):
        j = i // 2
        if i % 2 == 0:
            h = _attn_layer(h, mixer_norm[i].astype(F32), attn_w_qkv[j], attn_q_norm[j], attn_k_norm[j],
                            attn_lambda[j], attn_subln[j], attn_w_o[j], _lambda_init(i), tables,
                            batch=batch, seq=seq)
        else:
            h = _ssm_layer(h, mixer_norm[i].astype(F32), ssm_w_in[j], ssm_conv_w[j], ssm_conv_b[j],
                           ssm_dt_bias[j], ssm_a_log[j], ssm_d[j], ssm_norm[j], ssm_w_out[j],
                           batch=batch, seq=seq)
        h = _mlp_layer(h, mlp_norm[i].astype(F32), mlp_w1[i], mlp_w2[i])
    return h.reshape(batch, seq, d_model).astype(x.dtype)
```

```python
import functools
import math

import jax
import jax.numpy as jnp
from jax import lax
from jax.experimental import pallas as pl
from jax.experimental.pallas import tpu as pltpu

F32 = jnp.float32
BF16 = jnp.bfloat16

EPS = 1e-6
LANES = 128
SUBLANES = 8
VMEM_LIMIT_BYTES = 56 * 1024 * 1024

DA_HEADS = 8
DA_HEAD_DIM = 128
DA_V_DIM = 2 * DA_HEAD_DIM
DA_QK_WIDTH = DA_HEADS * 2 * DA_HEAD_DIM
DA_V_WIDTH = DA_HEADS * DA_V_DIM
ROPE_THETA = 500000.0
ROPE_DIM = DA_HEAD_DIM // 4
SSM_HEAD_DIM = 64
SSM_GROUPS = 8
SSM_HPG = 8
SSM_D_STATE = 128
SSM_CONV = 4
SSM_CHUNK = 128
SSM_GROUP_WIDTH = SSM_HPG * SSM_HEAD_DIM
SSM_D_INNER = SSM_GROUPS * SSM_GROUP_WIDTH
SSM_HEADS = SSM_GROUPS * SSM_HPG
SSM_BC_WIDTH = SSM_GROUPS * SSM_D_STATE
SSM_ZX_WIDTH = 2 * SSM_D_INNER + 2 * SSM_BC_WIDTH

LOG2_E = math.log2(math.e)
ACT_DTYPE = BF16


def _lambda_init(layer_idx):
    return 0.8 - 0.6 * math.exp(-0.3 * layer_idx)


def _sigmoid(v):
    return 1.0 / (1.0 + jnp.exp(-v))


def _softplus(v):
    return jnp.maximum(v, 0.0) + jnp.log(1.0 + jnp.exp(-jnp.abs(v)))


def _params(sem):
    return pltpu.CompilerParams(dimension_semantics=sem, vmem_limit_bytes=VMEM_LIMIT_BYTES)


def _pick(total, pref):
    t = min(total, pref)
    assert total % t == 0, (total, pref)
    return t


def _rope_group(v, gain, cosf, sina, sinb):
    ms = jnp.mean(v * v, axis=-1, keepdims=True)
    y = v * lax.rsqrt(ms + EPS) * gain
    up = pltpu.roll(y, LANES - ROPE_DIM // 2, axis=1)
    dn = pltpu.roll(y, ROPE_DIM // 2, axis=1)
    return y * cosf + up * sina + dn * sinb


def _norm_mm_kernel(x_ref, g_ref, w_ref, *rest, mode, tn, q_scale):
    if mode == "qkv":
        qg_ref, kg_ref, cos_ref, sa_ref, sb_ref, o_ref, xn_ref = rest
    else:
        o_ref, xn_ref = rest
    j = pl.program_id(1)

    @pl.when(j == 0)
    def _():
        x = x_ref[...]
        ms = jnp.mean(x * x, axis=-1, keepdims=True)
        xn_ref[...] = (x * lax.rsqrt(ms + EPS) * g_ref[...]).astype(BF16)

    acc = jnp.dot(xn_ref[...], w_ref[...], preferred_element_type=F32)
    if mode == "plain":
        o_ref[...] = acc.astype(o_ref.dtype)
    elif mode == "relu2":
        r = jnp.maximum(acc, 0.0)
        o_ref[...] = (r * r).astype(o_ref.dtype)
    else:
        n_q = DA_QK_WIDTH // tn

        @pl.when(j < 2 * n_q)
        def _():
            gain = jnp.where(j < n_q, qg_ref[...] * q_scale, kg_ref[...])
            cosf, sina, sinb = cos_ref[...], sa_ref[...], sb_ref[...]
            for g in range(tn // LANES):
                sl = slice(g * LANES, (g + 1) * LANES)
                o_ref[:, sl] = _rope_group(acc[:, sl], gain, cosf, sina, sinb).astype(o_ref.dtype)

        @pl.when(j >= 2 * n_q)
        def _():
            o_ref[...] = acc.astype(o_ref.dtype)


def _norm_matmul(x, gain, w, *, mode, out_dtype, tm_pref=1024, tn_pref=1024, qkv_extra=None, seq=None):
    t, k = x.shape
    n = w.shape[1]
    tm = _pick(seq if mode == "qkv" else t, tm_pref)
    tn = _pick(n, tn_pref)
    grid = (t // tm, n // tn)
    in_specs = [
        pl.BlockSpec((tm, k), lambda i, j: (i, 0)),
        pl.BlockSpec((1, k), lambda i, j: (0, 0)),
        pl.BlockSpec((k, tn), lambda i, j: (0, j)),
    ]
    args = [x, gain.reshape(1, k), w]
    if mode == "qkv":
        qg, kg, cosf, sina, sinb = qkv_extra
        assert seq % tm == 0 and DA_QK_WIDTH % tn == 0
        n_s = seq // tm
        vec = pl.BlockSpec((1, LANES), lambda i, j: (0, 0))
        tab = pl.BlockSpec((tm, LANES), lambda i, j: (i % n_s, 0))
        in_specs += [vec, vec, tab, tab, tab]
        args += [qg.reshape(1, LANES), kg.reshape(1, LANES), cosf, sina, sinb]
    return pl.pallas_call(
        functools.partial(_norm_mm_kernel, mode=mode, tn=tn, q_scale=DA_HEAD_DIM ** -0.5 * LOG2_E),
        grid=grid,
        in_specs=in_specs,
        out_specs=pl.BlockSpec((tm, tn), lambda i, j: (i, j)),
        out_shape=jax.ShapeDtypeStruct((t, n), out_dtype),
        scratch_shapes=[pltpu.VMEM((tm, k), BF16)],
        compiler_params=_params(("parallel", "arbitrary")),
        name="norm_mm_" + mode,
    )(*args)


def _mm_res_kernel(x_ref, w_ref, r_ref, o_ref, *scratch, nk):
    p = jnp.dot(x_ref[...], w_ref[...], preferred_element_type=F32)
    if nk == 1:
        o_ref[...] = r_ref[...] + p
        return
    acc_ref, = scratch
    k = pl.program_id(2)

    @pl.when(k == 0)
    def _():
        acc_ref[...] = p

    if nk > 2:
        @pl.when((k > 0) & (k < nk - 1))
        def _():
            acc_ref[...] += p

    @pl.when(k == nk - 1)
    def _():
        o_ref[...] = r_ref[...] + (acc_ref[...] + p)


def _matmul_residual(x, w, res, *, tm_pref=1024, tn_pref=1024, tk_pref=2048):
    t, k = x.shape
    n = w.shape[1]
    tm, tn, tk = _pick(t, tm_pref), _pick(n, tn_pref), _pick(k, tk_pref)
    nk = k // tk
    return pl.pallas_call(
        functools.partial(_mm_res_kernel, nk=nk),
        grid=(t // tm, n // tn, nk),
        in_specs=[
            pl.BlockSpec((tm, tk), lambda i, j, kk: (i, kk)),
            pl.BlockSpec((tk, tn), lambda i, j, kk: (kk, j)),
            pl.BlockSpec((tm, tn), lambda i, j, kk: (i, j)),
        ],
        out_specs=pl.BlockSpec((tm, tn), lambda i, j, kk: (i, j)),
        out_shape=jax.ShapeDtypeStruct((t, n), F32),
        scratch_shapes=[pltpu.VMEM((tm, tn), F32)] if nk > 1 else [],
        compiler_params=_params(("parallel", "parallel", "arbitrary")),
        name="mm_res",
    )(x, w, res)


NEG_BIG = -1e30
MAX_UNSHIFTED_SCORE = 64.0


def _attn_kernel(safe_ref, lam_ref, sg_ref, q_ref, k_ref, v_ref, o_ref, m_ref, l_ref, acc_ref, lp_ref,
                 *, tq, lam_init):
    qi = pl.program_id(2)
    d = DA_HEAD_DIM
    wide = 2 * tq
    safe = safe_ref[0] == 1

    acc_ref[...] = jnp.zeros(acc_ref.shape, F32)

    def scores(c, kb, width, diag_col):
        qc = q_ref[0, :, c * d:(c + 1) * d]
        s = lax.dot_general(qc, kb[:, c * d:(c + 1) * d], (((1,), (1,)), ((), ())),
                            preferred_element_type=F32)
        if diag_col is None:
            return s, None
        row = lax.broadcasted_iota(jnp.int32, (tq, width), 0)
        col = lax.broadcasted_iota(jnp.int32, (tq, width), 1)
        return s, col <= row + diag_col

    def step_unshifted(start, width, diag_col):
        kb = k_ref[0, pl.ds(start, width), :]
        vb = v_ref[0, pl.ds(start, width), :]
        for c in range(2):
            s, keep = scores(c, kb, width, diag_col)
            p = jnp.exp2(s)
            if keep is not None:
                p = jnp.where(keep, p, 0.0)
            part = p[:, 0:LANES]
            for g in range(1, width // LANES):
                part = part + p[:, g * LANES:(g + 1) * LANES]
            lp_ref[c] += part
            acc_ref[c] += jnp.dot(p.astype(BF16), vb, preferred_element_type=F32)

    def step_online(start, width, diag_col):
        kb = k_ref[0, pl.ds(start, width), :]
        vb = v_ref[0, pl.ds(start, width), :]
        for c in range(2):
            s, keep = scores(c, kb, width, diag_col)
            if keep is not None:
                s = jnp.where(keep, s, NEG_BIG)
            m_old = m_ref[c]
            m_new = jnp.maximum(m_old, jnp.max(s, axis=-1, keepdims=True))
            alpha = jnp.exp2(m_old - m_new)
            p = jnp.exp2(s - m_new)
            l_ref[c] = alpha * l_ref[c] + jnp.sum(p, axis=-1, keepdims=True)
            acc_ref[c] = alpha * acc_ref[c] + jnp.dot(p.astype(BF16), vb, preferred_element_type=F32)
            m_ref[c] = m_new

    def run(step):
        def body(j, carry):
            step(pl.multiple_of(j * wide, wide), wide, None)
            return carry

        lax.fori_loop(0, qi // 2, body, 0)

        @pl.when(qi % 2 == 1)
        def _():
            step(pl.multiple_of((qi - 1) * tq, tq), wide, tq)

        @pl.when(qi % 2 == 0)
        def _():
            step(pl.multiple_of(qi * tq, tq), tq, 0)

    @pl.when(safe)
    def _():
        lp_ref[...] = jnp.zeros(lp_ref.shape, F32)
        run(step_unshifted)
        l_ref[...] = jnp.sum(lp_ref[...], axis=-1, keepdims=True)

    @pl.when(jnp.logical_not(safe))
    def _():
        m_ref[...] = jnp.full(m_ref.shape, NEG_BIG, F32)
        l_ref[...] = jnp.zeros(l_ref.shape, F32)
        run(step_online)

    lam = lam_ref[...]
    lam_full = (jnp.exp(jnp.sum(lam[0:1] * lam[1:2], axis=-1, keepdims=True))
                - jnp.exp(jnp.sum(lam[2:3] * lam[3:4], axis=-1, keepdims=True)) + lam_init)
    o = acc_ref[0] / l_ref[0] - lam_full * (acc_ref[1] / l_ref[1])
    ms = jnp.mean(o * o, axis=-1, keepdims=True)
    o_ref[0] = (o * lax.rsqrt(ms + EPS) * (sg_ref[...] * (1.0 - lam_init))).astype(o_ref.dtype)


def _scores_are_bounded(q_gain, k_gain):
    bound = (jnp.max(jnp.abs(q_gain)) * jnp.max(jnp.abs(k_gain))
             * (DA_HEAD_DIM * DA_HEAD_DIM ** -0.5 * LOG2_E * 1.02))
    return (bound <= MAX_UNSHIFTED_SCORE).astype(jnp.int32).reshape(1)


def _diff_attention(qkv, safe, lam, subln, *, batch, seq, lam_init, tq_pref=512):
    tq = _pick(seq, tq_pref)
    hw = 2 * DA_HEAD_DIM
    kern = functools.partial(_attn_kernel, tq=tq, lam_init=lam_init)
    return pl.pallas_call(
        kern,
        grid=(batch, DA_HEADS, seq // tq),
        in_specs=[
            pl.BlockSpec(memory_space=pltpu.SMEM),
            pl.BlockSpec((4, DA_HEAD_DIM), lambda b, h, i: (0, 0)),
            pl.BlockSpec((1, DA_V_DIM), lambda b, h, i: (0, 0)),
            pl.BlockSpec((1, tq, hw), lambda b, h, i: (b, i, h)),
            pl.BlockSpec((1, seq, hw), lambda b, h, i: (b, 0, DA_HEADS + h)),
            pl.BlockSpec((1, seq, DA_V_DIM), lambda b, h, i: (b, 0, 2 * DA_HEADS + h)),
        ],
        out_specs=pl.BlockSpec((1, tq, DA_V_DIM), lambda b, h, i: (b, i, h)),
        out_shape=jax.ShapeDtypeStruct((batch, seq, DA_V_WIDTH), ACT_DTYPE),
        scratch_shapes=[
            pltpu.VMEM((2, tq, 1), F32),
            pltpu.VMEM((2, tq, 1), F32),
            pltpu.VMEM((2, tq, DA_V_DIM), F32),
            pltpu.VMEM((2, tq, LANES), F32),
        ],
        compiler_params=_params(("parallel", "parallel", "arbitrary")),
        name="diff_attn",
    )(safe, lam, subln.reshape(1, DA_V_DIM), qkv, qkv, qkv)


def _split3(v):
    hi = v.astype(BF16)
    r1 = v - hi.astype(F32)
    mid = r1.astype(BF16)
    lo = (r1 - mid.astype(F32)).astype(BF16)
    return hi, mid, lo


def _ssd_prep_kernel(dt_ref, bias_ref, alog_ref, a_ref, at_ref, dtt_ref, *, ts):
    L = SSM_CHUNK
    dt = _softplus(dt_ref[0] + bias_ref[...])
    dta = dt * (-jnp.exp(alog_ref[...]))
    row = lax.broadcasted_iota(jnp.int32, (L, L), 0)
    col = lax.broadcasted_iota(jnp.int32, (L, L), 1)
    tril = jnp.where(row >= col, 1.0, 0.0).astype(BF16)
    for ci in range(ts // L):
        sl = slice(ci * L, (ci + 1) * L)
        hi, mid, lo = _split3(dta[sl])
        a = (jnp.dot(tril, lo, preferred_element_type=F32) + jnp.dot(tril, mid, preferred_element_type=F32)
             + jnp.dot(tril, hi, preferred_element_type=F32))
        a_ref[0, sl, :] = a
        at_ref[0, :, sl] = a.T
        dtt_ref[0, :, sl] = dt[sl].T


def _ssd_prep(dt_raw, dt_bias, a_log, *, batch, seq, ts_pref=512):
    ts = _pick(seq, ts_pref)
    pad = LANES - SSM_HEADS
    bias = jnp.pad(dt_bias.astype(F32), (0, pad)).reshape(1, LANES)
    alog = jnp.pad(a_log.astype(F32), (0, pad)).reshape(1, LANES)
    col = pl.BlockSpec((1, ts, LANES), lambda b, c: (b, c, 0))
    rowl = pl.BlockSpec((1, LANES, ts), lambda b, c: (b, 0, c))
    vec = pl.BlockSpec((1, LANES), lambda b, c: (0, 0))
    return pl.pallas_call(
        functools.partial(_ssd_prep_kernel, ts=ts),
        grid=(batch, seq // ts),
        in_specs=[col, vec, vec],
        out_specs=[col, rowl, rowl],
        out_shape=[jax.ShapeDtypeStruct((batch, seq, LANES), F32),
                   jax.ShapeDtypeStruct((batch, LANES, seq), F32),
                   jax.ShapeDtypeStruct((batch, LANES, seq), F32)],
        compiler_params=_params(("parallel", "parallel")),
        name="ssd_prep",
    )(dt_raw.reshape(batch, seq, LANES), bias, alog)


def _ssd_kernel(x_ref, z_ref, b_ref, c_ref, cwx_ref, cwb_ref, cwc_ref, cbx_ref, cbb_ref, cbc_ref,
                ag_ref, at_ref, dtt_ref, d_ref, ng_ref, o_ref, xp_ref, bp_ref, cp_ref, st_ref, *, ts):
    L = SSM_CHUNK
    tail = SUBLANES
    cidx = pl.program_id(2)

    @pl.when(cidx == 0)
    def _():
        xp_ref[0:tail, :] = jnp.zeros((tail, xp_ref.shape[1]), F32)
        bp_ref[0:tail, :] = jnp.zeros((tail, bp_ref.shape[1]), F32)
        cp_ref[0:tail, :] = jnp.zeros((tail, cp_ref.shape[1]), F32)
        st_ref[...] = jnp.zeros(st_ref.shape, F32)

    xp_ref[tail:tail + ts, :] = x_ref[0].astype(F32)
    bp_ref[tail:tail + ts, :] = b_ref[0].astype(F32)
    cp_ref[tail:tail + ts, :] = c_ref[0].astype(F32)

    row = lax.broadcasted_iota(jnp.int32, (L, L), 0)
    col = lax.broadcasted_iota(jnp.int32, (L, L), 1)
    causal = row >= col
    low_half = lax.broadcasted_iota(jnp.int32, (1, LANES), 1) < SSM_HEAD_DIM

    def conv_silu(pad_ref, w_ref, bias_ref, r0):
        acc = bias_ref[...]
        for k in range(SSM_CONV):
            off = r0 + tail - (SSM_CONV - 1) + k
            acc = acc + w_ref[k:k + 1, :] * pad_ref[off:off + L, :]
        return acc * _sigmoid(acc)

    for ci in range(ts // L):
        r0 = ci * L
        xc = conv_silu(xp_ref, cwx_ref, cbx_ref, r0)
        bc = conv_silu(bp_ref, cwb_ref, cbb_ref, r0)
        cc = conv_silu(cp_ref, cwc_ref, cbc_ref, r0)
        cb = lax.dot_general(cc.astype(BF16), bc.astype(BF16), (((1,), (1,)), ((), ())),
                             preferred_element_type=F32)
        bt = bc.T
        acol = ag_ref[0, 0, r0:r0 + L, :]
        arow = at_ref[0, :, r0:r0 + L]
        dtrow = dtt_ref[0, :, r0:r0 + L]
        wrow = dtrow * jnp.exp(arow[:, L - 1:L] - arow)
        st = st_ref[...]
        xcb = xc.astype(BF16)
        stb = st.astype(BF16)
        ys, news = [], []
        for j in range(SSM_HPG // 2):
            sl = slice(j * LANES, (j + 1) * LANES)
            xpair = xcb[:, sl]
            rhs = jnp.concatenate([xpair, stb[:, sl]], axis=0)
            y2, s2, d2 = [], [], []
            for hh in range(2):
                h = 2 * j + hh
                ac = acol[:, h:h + 1]
                seg = ac - arow[h:h + 1, :]
                dec = jnp.exp(jnp.where(causal, seg, -jnp.inf))
                m = (cb * dec * dtrow[h:h + 1, :]).astype(BF16)
                cs = (cc * jnp.exp(ac)).astype(BF16)
                lhs = jnp.concatenate([m, cs], axis=1)
                y2.append(jnp.dot(lhs, rhs, preferred_element_type=F32))
                bw = (bt * wrow[h:h + 1, :]).astype(BF16)
                s2.append(jnp.dot(bw, xpair, preferred_element_type=F32))
                d2.append(jnp.exp(acol[L - 1:L, h:h + 1]))
            ys.append(jnp.where(low_half, y2[0], y2[1]))
            news.append(st[:, sl] * jnp.where(low_half, d2[0], d2[1]) + jnp.where(low_half, s2[0], s2[1]))
        st_ref[...] = jnp.concatenate(news, axis=1)
        y = jnp.concatenate(ys, axis=1) + d_ref[...] * xc
        zz = z_ref[0, r0:r0 + L, :].astype(F32)
        y = y * (zz * _sigmoid(zz))
        ms = jnp.mean(y * y, axis=-1, keepdims=True)
        o_ref[0, r0:r0 + L, :] = (y * lax.rsqrt(ms + EPS) * ng_ref[...]).astype(o_ref.dtype)

    xp_ref[0:tail, :] = xp_ref[ts:ts + tail, :]
    bp_ref[0:tail, :] = bp_ref[ts:ts + tail, :]
    cp_ref[0:tail, :] = cp_ref[ts:ts + tail, :]


def _ssd(zx, a_g, a_t, dt_t, conv_w, conv_b, d_skip, norm_gain, *, batch, seq, ts_pref=256):
    ts = _pick(seq, ts_pref)
    gw, n, G = SSM_GROUP_WIDTH, SSM_D_STATE, SSM_GROUPS
    x_blk0 = SSM_D_INNER // gw
    b_blk0 = 2 * SSM_D_INNER // n
    c_blk0 = b_blk0 + SSM_BC_WIDTH // n
    cw = conv_w.astype(F32)
    cbias = conv_b.astype(F32).reshape(1, -1)
    d_exp = jnp.repeat(d_skip.astype(F32), SSM_HEAD_DIM).reshape(1, SSM_D_INNER)
    ng = norm_gain.astype(F32).reshape(1, SSM_D_INNER)
    cwb0 = SSM_D_INNER // n
    cwc0 = cwb0 + SSM_BC_WIDTH // n
    in_specs = [
        pl.BlockSpec((1, ts, gw), lambda b, g, c: (b, c, x_blk0 + g)),
        pl.BlockSpec((1, ts, gw), lambda b, g, c: (b, c, g)),
        pl.BlockSpec((1, ts, n), lambda b, g, c: (b, c, b_blk0 + g)),
        pl.BlockSpec((1, ts, n), lambda b, g, c: (b, c, c_blk0 + g)),
        pl.BlockSpec((SSM_CONV, gw), lambda b, g, c: (0, g)),
        pl.BlockSpec((SSM_CONV, n), lambda b, g, c: (0, cwb0 + g)),
        pl.BlockSpec((SSM_CONV, n), lambda b, g, c: (0, cwc0 + g)),
        pl.BlockSpec((1, gw), lambda b, g, c: (0, g)),
        pl.BlockSpec((1, n), lambda b, g, c: (0, cwb0 + g)),
        pl.BlockSpec((1, n), lambda b, g, c: (0, cwc0 + g)),
        pl.BlockSpec((1, 1, ts, SSM_HPG), lambda b, g, c: (b, g, c, 0)),
        pl.BlockSpec((1, SSM_HPG, ts), lambda b, g, c: (b, g, c)),
        pl.BlockSpec((1, SSM_HPG, ts), lambda b, g, c: (b, g, c)),
        pl.BlockSpec((1, gw), lambda b, g, c: (0, g)),
        pl.BlockSpec((1, gw), lambda b, g, c: (0, g)),
    ]
    return pl.pallas_call(
        functools.partial(_ssd_kernel, ts=ts),
        grid=(batch, G, seq // ts),
        in_specs=in_specs,
        out_specs=pl.BlockSpec((1, ts, gw), lambda b, g, c: (b, c, g)),
        out_shape=jax.ShapeDtypeStruct((batch, seq, SSM_D_INNER), ACT_DTYPE),
        scratch_shapes=[
            pltpu.VMEM((ts + SUBLANES, gw), F32),
            pltpu.VMEM((ts + SUBLANES, n), F32),
            pltpu.VMEM((ts + SUBLANES, n), F32),
            pltpu.VMEM((n, gw), F32),
        ],
        compiler_params=_params(("parallel", "parallel", "arbitrary")),
        name="ssd_scan",
    )(zx, zx, zx, zx, cw, cw, cw, cbias, cbias, cbias, a_g, a_t, dt_t, d_exp, ng)


def _rope_tables(seq):
    half = ROPE_DIM // 2
    inv = ROPE_THETA ** (-jnp.arange(0, ROPE_DIM, 2, dtype=F32) / ROPE_DIM)
    ang = jnp.arange(seq, dtype=F32)[:, None] * inv[None, :]
    cos, sin = jnp.cos(ang), jnp.sin(ang)
    zeros = jnp.zeros((seq, LANES - half), F32)
    cosf = jnp.concatenate([cos, cos, jnp.ones((seq, LANES - ROPE_DIM), F32)], axis=1)
    sina = jnp.concatenate([-sin, zeros], axis=1)
    sinb = jnp.concatenate([jnp.zeros((seq, half), F32), sin, zeros[:, half:]], axis=1)
    return cosf, sina, sinb


def _attn_layer(h, norm_g, w_qkv, q_gain, k_gain, lam, subln, w_o, lam_init, tables, *, batch, seq):
    qkv = _norm_matmul(h, norm_g, w_qkv.astype(BF16), mode="qkv", out_dtype=ACT_DTYPE,
                       qkv_extra=(q_gain.astype(F32), k_gain.astype(F32)) + tables, seq=seq)
    safe = _scores_are_bounded(q_gain.astype(F32), k_gain.astype(F32))
    o = _diff_attention(qkv.reshape(batch, seq, -1), safe, lam.astype(F32), subln.astype(F32),
                        batch=batch, seq=seq, lam_init=lam_init)
    return _matmul_residual(o.reshape(batch * seq, DA_V_WIDTH), w_o.astype(BF16), h)


def _ssm_layer(h, norm_g, w_in, conv_w, conv_b, dt_bias, a_log, d_skip, norm_gain, w_out, *, batch, seq):
    w_zx = w_in[:, :SSM_ZX_WIDTH].astype(BF16)
    w_dt = jnp.pad(w_in[:, SSM_ZX_WIDTH:], ((0, 0), (0, LANES - SSM_HEADS))).astype(BF16)
    zx = _norm_matmul(h, norm_g, w_zx, mode="plain", out_dtype=ACT_DTYPE)
    dt_raw = _norm_matmul(h, norm_g, w_dt, mode="plain", out_dtype=F32)
    a_c, a_t, dt_t = _ssd_prep(dt_raw, dt_bias, a_log, batch=batch, seq=seq)
    a_g = a_c[:, :, :SSM_HEADS].reshape(batch, seq, SSM_GROUPS, SSM_HPG).transpose(0, 2, 1, 3)
    y = _ssd(zx.reshape(batch, seq, SSM_ZX_WIDTH), a_g, a_t, dt_t, conv_w, conv_b, d_skip, norm_gain,
             batch=batch, seq=seq)
    return _matmul_residual(y.reshape(batch * seq, SSM_D_INNER), w_out.astype(BF16), h)


def _mlp_layer(h, norm_g, w1, w2):
    a = _norm_matmul(h, norm_g, w1.astype(BF16), mode="relu2", out_dtype=ACT_DTYPE)
    return _matmul_residual(a, w2.astype(BF16), h)


def kernel(x, mixer_norm, mlp_norm, attn_w_qkv, attn_q_norm, attn_k_norm, attn_lambda, attn_subln, attn_w_o, ssm_w_in, ssm_conv_w, ssm_conv_b, ssm_dt_bias, ssm_a_log, ssm_d, ssm_norm, ssm_w_out, mlp_w1, mlp_w2):
    batch, seq, d_model = x.shape
    depth = mixer_norm.shape[0]
    h = x.reshape(batch * seq, d_model).astype(F32)
    tables = _rope_tables(seq)
    for i in range(depth):
        j = i // 2
        if i % 2 == 0:
            h = _attn_layer(h, mixer_norm[i].astype(F32), attn_w_qkv[j], attn_q_norm[j], attn_k_norm[j],
                            attn_lambda[j], attn_subln[j], attn_w_o[j], _lambda_init(i), tables,
                            batch=batch, seq=seq)
        else:
            h = _ssm_layer(h, mixer_norm[i].astype(F32), ssm_w_in[j], ssm_conv_w[j], ssm_conv_b[j],
                           ssm_dt_bias[j], ssm_a_log[j], ssm_d[j], ssm_norm[j], ssm_w_out[j],
                           batch=batch, seq=seq)
        h = _mlp_layer(h, mlp_norm[i].astype(F32), mlp_w1[i], mlp_w2[i])
    return h.reshape(batch, seq, d_model).astype(x.dtype)
```

```python
import functools
import math

import jax
import jax.numpy as jnp
from jax import lax
from jax.experimental import pallas as pl
from jax.experimental.pallas import tpu as pltpu

F32 = jnp.float32
BF16 = jnp.bfloat16

EPS = 1e-6
LANES = 128
SUBLANES = 8
VMEM_LIMIT_BYTES = 56 * 1024 * 1024

DA_HEADS = 8
DA_HEAD_DIM = 128
DA_V_DIM = 2 * DA_HEAD_DIM
DA_QK_WIDTH = DA_HEADS * 2 * DA_HEAD_DIM
DA_V_WIDTH = DA_HEADS * DA_V_DIM
ROPE_THETA = 500000.0
ROPE_DIM = DA_HEAD_DIM // 4
SSM_HEAD_DIM = 64
SSM_GROUPS = 8
SSM_HPG = 8
SSM_D_STATE = 128
SSM_CONV = 4
SSM_CHUNK = 128
SSM_GROUP_WIDTH = SSM_HPG * SSM_HEAD_DIM
SSM_D_INNER = SSM_GROUPS * SSM_GROUP_WIDTH
SSM_HEADS = SSM_GROUPS * SSM_HPG
SSM_BC_WIDTH = SSM_GROUPS * SSM_D_STATE
SSM_ZX_WIDTH = 2 * SSM_D_INNER + 2 * SSM_BC_WIDTH

LOG2_E = math.log2(math.e)
ACT_DTYPE = BF16


def _lambda_init(layer_idx):
    return 0.8 - 0.6 * math.exp(-0.3 * layer_idx)


def _sigmoid(v):
    return 1.0 / (1.0 + jnp.exp(-v))


def _softplus(v):
    return jnp.maximum(v, 0.0) + jnp.log(1.0 + jnp.exp(-jnp.abs(v)))


def _params(sem):
    return pltpu.CompilerParams(dimension_semantics=sem, vmem_limit_bytes=VMEM_LIMIT_BYTES)


def _pick(total, pref):
    t = min(total, pref)
    assert total % t == 0, (total, pref)
    return t


def _rope_group(v, gain, cosf, sins):
    ms = jnp.mean(v * v, axis=-1, keepdims=True)
    y = v * lax.rsqrt(ms + EPS) * gain
    return y * cosf + pltpu.roll(y, LANES // 2, axis=1) * sins


def _norm_mm_kernel(x_ref, g_ref, w_ref, *rest, mode, tn, q_scale):
    if mode == "qkv":
        qg_ref, kg_ref, cos_ref, sin_ref, o_ref, xn_ref = rest
    else:
        o_ref, xn_ref = rest
    j = pl.program_id(1)

    @pl.when(j == 0)
    def _():
        x = x_ref[...]
        ms = jnp.mean(x * x, axis=-1, keepdims=True)
        xn_ref[...] = (x * lax.rsqrt(ms + EPS) * g_ref[...]).astype(BF16)

    acc = jnp.dot(xn_ref[...], w_ref[...], preferred_element_type=F32)
    if mode == "plain":
        o_ref[...] = acc.astype(o_ref.dtype)
    elif mode == "relu2":
        r = jnp.maximum(acc, 0.0)
        o_ref[...] = (r * r).astype(o_ref.dtype)
    else:
        n_q = DA_QK_WIDTH // tn

        @pl.when(j < 2 * n_q)
        def _():
            gain = jnp.where(j < n_q, qg_ref[...] * q_scale, kg_ref[...])
            cosf, sins = cos_ref[...], sin_ref[...]
            for g in range(tn // LANES):
                sl = slice(g * LANES, (g + 1) * LANES)
                o_ref[:, sl] = _rope_group(acc[:, sl], gain, cosf, sins).astype(o_ref.dtype)

        @pl.when(j >= 2 * n_q)
        def _():
            o_ref[...] = acc.astype(o_ref.dtype)


def _norm_matmul(x, gain, w, layer, *, mode, out_dtype, n_out=None, tm_pref=1024, tn_pref=1024,
                 qkv_extra=None, seq=None):
    t, k = x.shape
    n = w.shape[2] if n_out is None else n_out
    tm = _pick(seq if mode == "qkv" else t, tm_pref)
    tn = _pick(n, tn_pref)
    grid = (t // tm, n // tn)
    in_specs = [
        pl.BlockSpec((tm, k), lambda i, j: (i, 0)),
        pl.BlockSpec((1, k), lambda i, j: (0, 0)),
        pl.BlockSpec((None, k, tn), lambda i, j: (layer, 0, j)),
    ]
    args = [x, gain.reshape(1, k), w]
    if mode == "qkv":
        qg, kg, cosf, sins = qkv_extra
        assert seq % tm == 0 and DA_QK_WIDTH % tn == 0
        n_s = seq // tm
        vec = pl.BlockSpec((1, LANES), lambda i, j: (0, 0))
        tab = pl.BlockSpec((tm, LANES), lambda i, j: (i % n_s, 0))
        in_specs += [vec, vec, tab, tab]
        args += [qg.reshape(1, LANES), kg.reshape(1, LANES), cosf, sins]
    return pl.pallas_call(
        functools.partial(_norm_mm_kernel, mode=mode, tn=tn, q_scale=DA_HEAD_DIM ** -0.5 * LOG2_E),
        grid=grid,
        in_specs=in_specs,
        out_specs=pl.BlockSpec((tm, tn), lambda i, j: (i, j)),
        out_shape=jax.ShapeDtypeStruct((t, n), out_dtype),
        scratch_shapes=[pltpu.VMEM((tm, k), BF16)],
        compiler_params=_params(("parallel", "arbitrary")),
        name="norm_mm_" + mode,
    )(*args)


def _mm_res_kernel(x_ref, w_ref, r_ref, o_ref, *scratch, nk):
    p = jnp.dot(x_ref[...], w_ref[...], preferred_element_type=F32)
    if nk == 1:
        o_ref[...] = r_ref[...] + p
        return
    acc_ref, = scratch
    k = pl.program_id(2)

    @pl.when(k == 0)
    def _():
        acc_ref[...] = p

    if nk > 2:
        @pl.when((k > 0) & (k < nk - 1))
        def _():
            acc_ref[...] += p

    @pl.when(k == nk - 1)
    def _():
        o_ref[...] = r_ref[...] + (acc_ref[...] + p)


def _matmul_residual(x, w, layer, res, *, tm_pref=1024, tn_pref=1024, tk_pref=2048):
    t, k = x.shape
    n = w.shape[2]
    tm, tn, tk = _pick(t, tm_pref), _pick(n, tn_pref), _pick(k, tk_pref)
    nk = k // tk
    return pl.pallas_call(
        functools.partial(_mm_res_kernel, nk=nk),
        grid=(t // tm, n // tn, nk),
        in_specs=[
            pl.BlockSpec((tm, tk), lambda i, j, kk: (i, kk)),
            pl.BlockSpec((None, tk, tn), lambda i, j, kk: (layer, kk, j)),
            pl.BlockSpec((tm, tn), lambda i, j, kk: (i, j)),
        ],
        out_specs=pl.BlockSpec((tm, tn), lambda i, j, kk: (i, j)),
        out_shape=jax.ShapeDtypeStruct((t, n), F32),
        scratch_shapes=[pltpu.VMEM((tm, tn), F32)] if nk > 1 else [],
        compiler_params=_params(("parallel", "parallel", "arbitrary")),
        name="mm_res",
    )(x, w, res)


NEG_BIG = -1e30
MAX_UNSHIFTED_SCORE = 64.0


def _attn_kernel(safe_ref, lam_ref, sg_ref, q_ref, k_ref, v_ref, o_ref, m_ref, l_ref, acc_ref, lp_ref,
                 *, tq, lam_init):
    qi = pl.program_id(2)
    d = DA_HEAD_DIM
    wide = 2 * tq
    safe = safe_ref[0] == 1

    acc_ref[...] = jnp.zeros(acc_ref.shape, F32)

    def scores(c, kb, width, diag_col):
        qc = q_ref[0, :, c * d:(c + 1) * d]
        s = lax.dot_general(qc, kb[:, c * d:(c + 1) * d], (((1,), (1,)), ((), ())),
                            preferred_element_type=F32)
        if diag_col is None:
            return s, None
        row = lax.broadcasted_iota(jnp.int32, (tq, width), 0)
        col = lax.broadcasted_iota(jnp.int32, (tq, width), 1)
        return s, col <= row + diag_col

    def step_unshifted(start, width, diag_col):
        kb = k_ref[0, pl.ds(start, width), :]
        vb = v_ref[0, pl.ds(start, width), :]
        for c in range(2):
            s, keep = scores(c, kb, width, diag_col)
            p = jnp.exp2(s)
            if keep is not None:
                p = jnp.where(keep, p, 0.0)
            part = p[:, 0:LANES]
            for g in range(1, width // LANES):
                part = part + p[:, g * LANES:(g + 1) * LANES]
            lp_ref[c] += part
            acc_ref[c] += jnp.dot(p.astype(BF16), vb, preferred_element_type=F32)

    def step_online(start, width, diag_col):
        kb = k_ref[0, pl.ds(start, width), :]
        vb = v_ref[0, pl.ds(start, width), :]
        for c in range(2):
            s, keep = scores(c, kb, width, diag_col)
            if keep is not None:
                s = jnp.where(keep, s, NEG_BIG)
            m_old = m_ref[c]
            m_new = jnp.maximum(m_old, jnp.max(s, axis=-1, keepdims=True))
            alpha = jnp.exp2(m_old - m_new)
            p = jnp.exp2(s - m_new)
            l_ref[c] = alpha * l_ref[c] + jnp.sum(p, axis=-1, keepdims=True)
            acc_ref[c] = alpha * acc_ref[c] + jnp.dot(p.astype(BF16), vb, preferred_element_type=F32)
            m_ref[c] = m_new

    def run(step):
        def body(j, carry):
            step(pl.multiple_of(j * wide, wide), wide, None)
            return carry

        lax.fori_loop(0, qi // 2, body, 0)

        @pl.when(qi % 2 == 1)
        def _():
            step(pl.multiple_of((qi - 1) * tq, tq), wide, tq)

        @pl.when(qi % 2 == 0)
        def _():
            step(pl.multiple_of(qi * tq, tq), tq, 0)

    @pl.when(safe)
    def _():
        lp_ref[...] = jnp.zeros(lp_ref.shape, F32)
        run(step_unshifted)
        l_ref[...] = jnp.sum(lp_ref[...], axis=-1, keepdims=True)

    @pl.when(jnp.logical_not(safe))
    def _():
        m_ref[...] = jnp.full(m_ref.shape, NEG_BIG, F32)
        l_ref[...] = jnp.zeros(l_ref.shape, F32)
        run(step_online)

    lam = lam_ref[...]
    lam_full = (jnp.exp(jnp.sum(lam[0:1] * lam[1:2], axis=-1, keepdims=True))
                - jnp.exp(jnp.sum(lam[2:3] * lam[3:4], axis=-1, keepdims=True)) + lam_init)
    o = acc_ref[0] / l_ref[0] - lam_full * (acc_ref[1] / l_ref[1])
    ms = jnp.mean(o * o, axis=-1, keepdims=True)
    o_ref[0] = (o * lax.rsqrt(ms + EPS) * (sg_ref[...] * (1.0 - lam_init))).astype(o_ref.dtype)


def _scores_are_bounded(q_gain, k_gain):
    bound = (jnp.max(jnp.abs(q_gain)) * jnp.max(jnp.abs(k_gain))
             * (DA_HEAD_DIM * DA_HEAD_DIM ** -0.5 * LOG2_E * 1.02))
    return (bound <= MAX_UNSHIFTED_SCORE).astype(jnp.int32).reshape(1)


def _diff_attention(qkv, safe, lam, subln, *, batch, seq, lam_init, tq_pref=512):
    tq = _pick(seq, tq_pref)
    hw = 2 * DA_HEAD_DIM
    kern = functools.partial(_attn_kernel, tq=tq, lam_init=lam_init)
    return pl.pallas_call(
        kern,
        grid=(batch, DA_HEADS, seq // tq),
        in_specs=[
            pl.BlockSpec(memory_space=pltpu.SMEM),
            pl.BlockSpec((4, DA_HEAD_DIM), lambda b, h, i: (0, 0)),
            pl.BlockSpec((1, DA_V_DIM), lambda b, h, i: (0, 0)),
            pl.BlockSpec((1, tq, hw), lambda b, h, i: (b, i, h)),
            pl.BlockSpec((1, seq, hw), lambda b, h, i: (b, 0, DA_HEADS + h)),
            pl.BlockSpec((1, seq, DA_V_DIM), lambda b, h, i: (b, 0, 2 * DA_HEADS + h)),
        ],
        out_specs=pl.BlockSpec((1, tq, DA_V_DIM), lambda b, h, i: (b, i, h)),
        out_shape=jax.ShapeDtypeStruct((batch, seq, DA_V_WIDTH), ACT_DTYPE),
        scratch_shapes=[
            pltpu.VMEM((2, tq, 1), F32),
            pltpu.VMEM((2, tq, 1), F32),
            pltpu.VMEM((2, tq, DA_V_DIM), F32),
            pltpu.VMEM((2, tq, LANES), F32),
        ],
        compiler_params=_params(("parallel", "parallel", "arbitrary")),
        name="diff_attn",
    )(safe, lam, subln.reshape(1, DA_V_DIM), qkv, qkv, qkv)


def _split3(v):
    hi = v.astype(BF16)
    r1 = v - hi.astype(F32)
    mid = r1.astype(BF16)
    lo = (r1 - mid.astype(F32)).astype(BF16)
    return hi, mid, lo


def _ssd_prep_kernel(dt_ref, bias_ref, alog_ref, a2g_ref, b2g_ref, b2t_ref, *, ts):
    L = SSM_CHUNK
    dt = _softplus(dt_ref[0] + bias_ref[...])
    dta2 = dt * (-LOG2_E * jnp.exp(alog_ref[...]))
    ld2 = jnp.log(dt) * LOG2_E
    row = lax.broadcasted_iota(jnp.int32, (L, L), 0)
    col = lax.broadcasted_iota(jnp.int32, (L, L), 1)
    tril = jnp.where(row >= col, 1.0, 0.0).astype(BF16)
    for ci in range(ts // L):
        sl = slice(ci * L, (ci + 1) * L)
        hi, mid, lo = _split3(dta2[sl])
        a2 = (jnp.dot(tril, lo, preferred_element_type=F32) + jnp.dot(tril, mid, preferred_element_type=F32)
              + jnp.dot(tril, hi, preferred_element_type=F32))
        b2 = a2 - ld2[sl]
        b2t_ref[0, :, sl] = b2.T
        for g in range(SSM_GROUPS):
            gl = slice(g * SSM_HPG, (g + 1) * SSM_HPG)
            a2g_ref[0, g, sl, :] = a2[:, gl]
            b2g_ref[0, g, sl, :] = b2[:, gl]


def _ssd_prep(dt_raw, dt_bias, a_log, *, batch, seq, ts_pref=512):
    ts = _pick(seq, ts_pref)
    pad = LANES - SSM_HEADS
    bias = jnp.pad(dt_bias.astype(F32), (0, pad)).reshape(1, LANES)
    alog = jnp.pad(a_log.astype(F32), (0, pad)).reshape(1, LANES)
    col = pl.BlockSpec((1, ts, LANES), lambda b, c: (b, c, 0))
    grp = pl.BlockSpec((1, SSM_GROUPS, ts, SSM_HPG), lambda b, c: (b, 0, c, 0))
    rowl = pl.BlockSpec((1, LANES, ts), lambda b, c: (b, 0, c))
    vec = pl.BlockSpec((1, LANES), lambda b, c: (0, 0))
    grp_shape = jax.ShapeDtypeStruct((batch, SSM_GROUPS, seq, SSM_HPG), F32)
    return pl.pallas_call(
        functools.partial(_ssd_prep_kernel, ts=ts),
        grid=(batch, seq // ts),
        in_specs=[col, vec, vec],
        out_specs=[grp, grp, rowl],
        out_shape=[grp_shape, grp_shape, jax.ShapeDtypeStruct((batch, LANES, seq), F32)],
        compiler_params=_params(("parallel", "parallel")),
        name="ssd_prep",
    )(dt_raw.reshape(batch, seq, LANES), bias, alog)


def _ssd_kernel(x_ref, z_ref, b_ref, c_ref, cwx_ref, cwb_ref, cwc_ref, cbx_ref, cbb_ref, cbc_ref,
                a2g_ref, b2g_ref, b2t_ref, d_ref, ng_ref, o_ref, tail_ref, st_ref, *, ts):
    L = SSM_CHUNK
    gw, n = SSM_GROUP_WIDTH, SSM_D_STATE
    cidx = pl.program_id(2)

    slot = cidx % 2

    @pl.when(cidx == 0)
    def _():
        tail_ref[0] = jnp.zeros(tail_ref.shape[1:], tail_ref.dtype)
        st_ref[...] = jnp.zeros(st_ref.shape, F32)

    def rows(lo, hi):
        return jnp.concatenate([x_ref[0, lo:hi, :], b_ref[0, lo:hi, :], c_ref[0, lo:hi, :]], axis=1)

    row = lax.broadcasted_iota(jnp.int32, (L, L), 0)
    col = lax.broadcasted_iota(jnp.int32, (L, L), 1)
    causal = row >= col
    low_half = lax.broadcasted_iota(jnp.int32, (1, LANES), 1) < SSM_HEAD_DIM

    n_shift = SSM_CONV - 1
    srow = lax.broadcasted_iota(jnp.int32, (n_shift * L, 2 * L), 0)
    scol = lax.broadcasted_iota(jnp.int32, (n_shift * L, 2 * L), 1)
    shift = jnp.where(scol == (srow % L) + (L - n_shift) + srow // L, 1.0, 0.0).astype(BF16)
    conv_w = jnp.concatenate([cwx_ref[...], cwb_ref[...], cwc_ref[...]], axis=1)
    conv_b = jnp.concatenate([cbx_ref[...], cbb_ref[...], cbc_ref[...]], axis=1)

    for ci in range(ts // L):
        r0 = ci * L
        win = jnp.concatenate([tail_ref[slot], rows(0, L)], axis=0) if ci == 0 else rows(r0 - L, r0 + L)
        lagged = jnp.dot(shift, win, preferred_element_type=F32)
        acc = conv_b + conv_w[n_shift:n_shift + 1, :] * win[L:2 * L, :].astype(F32)
        for k in range(n_shift):
            acc = acc + conv_w[k:k + 1, :] * lagged[k * L:(k + 1) * L, :]
        act = acc * _sigmoid(acc)
        xc, bc, cc = act[:, 0:gw], act[:, gw:gw + n], act[:, gw + n:gw + 2 * n]
        cb = lax.dot_general(cc.astype(BF16), bc.astype(BF16), (((1,), (1,)), ((), ())),
                             preferred_element_type=F32)
        cbb = cb.astype(BF16)
        btb = bc.T.astype(BF16)
        a2c = a2g_ref[0, 0, r0:r0 + L, :]
        b2c = b2g_ref[0, 0, r0:r0 + L, :]
        b2r = b2t_ref[0, :, r0:r0 + L]
        a2last = a2c[L - 1:L, :]
        eac = jnp.exp2(a2c)
        wc = jnp.exp2(a2last - b2c)
        dlast = jnp.exp2(a2last)
        st = st_ref[...]
        xcb = xc.astype(BF16)
        stb = st.astype(BF16)
        ys, news = [], []
        for j in range(SSM_HPG // 2):
            sl = slice(j * LANES, (j + 1) * LANES)
            rhs = jnp.concatenate([xcb[:, sl], stb[:, sl]], axis=0)
            y2 = []
            for h in (2 * j, 2 * j + 1):
                seg = a2c[:, h:h + 1] - b2r[h:h + 1, :]
                e = jnp.exp2(jnp.where(causal, seg, -jnp.inf)).astype(BF16)
                cs = (cc * eac[:, h:h + 1]).astype(BF16)
                lhs = jnp.concatenate([cbb * e, cs], axis=1)
                y2.append(jnp.dot(lhs, rhs, preferred_element_type=F32))
            ys.append(jnp.where(low_half, y2[0], y2[1]))
            wpair = jnp.where(low_half, wc[:, 2 * j:2 * j + 1], wc[:, 2 * j + 1:2 * j + 2])
            dpair = jnp.where(low_half, dlast[:, 2 * j:2 * j + 1], dlast[:, 2 * j + 1:2 * j + 2])
            xw = (xc[:, sl] * wpair).astype(BF16)
            news.append(st[:, sl] * dpair + jnp.dot(btb, xw, preferred_element_type=F32))
        st_ref[...] = jnp.concatenate(news, axis=1)
        y = jnp.concatenate(ys, axis=1) + d_ref[...] * xc
        zz = z_ref[0, r0:r0 + L, :].astype(F32)
        y = y * (zz * _sigmoid(zz))
        ms = jnp.mean(y * y, axis=-1, keepdims=True)
        o_ref[0, r0:r0 + L, :] = (y * lax.rsqrt(ms + EPS) * ng_ref[...]).astype(o_ref.dtype)

    tail_ref[1 - slot] = rows(ts - L, ts)


def _ssd(zx, a2_g, b2_g, b2_t, conv_w, conv_b, d_skip, norm_gain, *, batch, seq, ts_pref=512):
    assert zx.dtype == BF16, "the shift-matrix conv is exact only for bf16 inputs"
    ts = _pick(seq, ts_pref)
    gw, n, G = SSM_GROUP_WIDTH, SSM_D_STATE, SSM_GROUPS
    x_blk0 = SSM_D_INNER // gw
    b_blk0 = 2 * SSM_D_INNER // n
    c_blk0 = b_blk0 + SSM_BC_WIDTH // n
    cw = conv_w.astype(F32)
    cbias = conv_b.astype(F32).reshape(1, -1)
    d_exp = jnp.repeat(d_skip.astype(F32), SSM_HEAD_DIM).reshape(1, SSM_D_INNER)
    ng = norm_gain.astype(F32).reshape(1, SSM_D_INNER)
    cwb0 = SSM_D_INNER // n
    cwc0 = cwb0 + SSM_BC_WIDTH // n
    in_specs = [
        pl.BlockSpec((1, ts, gw), lambda b, g, c: (b, c, x_blk0 + g)),
        pl.BlockSpec((1, ts, gw), lambda b, g, c: (b, c, g)),
        pl.BlockSpec((1, ts, n), lambda b, g, c: (b, c, b_blk0 + g)),
        pl.BlockSpec((1, ts, n), lambda b, g, c: (b, c, c_blk0 + g)),
        pl.BlockSpec((SSM_CONV, gw), lambda b, g, c: (0, g)),
        pl.BlockSpec((SSM_CONV, n), lambda b, g, c: (0, cwb0 + g)),
        pl.BlockSpec((SSM_CONV, n), lambda b, g, c: (0, cwc0 + g)),
        pl.BlockSpec((1, gw), lambda b, g, c: (0, g)),
        pl.BlockSpec((1, n), lambda b, g, c: (0, cwb0 + g)),
        pl.BlockSpec((1, n), lambda b, g, c: (0, cwc0 + g)),
        pl.BlockSpec((1, 1, ts, SSM_HPG), lambda b, g, c: (b, g, c, 0)),
        pl.BlockSpec((1, 1, ts, SSM_HPG), lambda b, g, c: (b, g, c, 0)),
        pl.BlockSpec((1, SSM_HPG, ts), lambda b, g, c: (b, g, c)),
        pl.BlockSpec((1, gw), lambda b, g, c: (0, g)),
        pl.BlockSpec((1, gw), lambda b, g, c: (0, g)),
    ]
    return pl.pallas_call(
        functools.partial(_ssd_kernel, ts=ts),
        grid=(batch, G, seq // ts),
        in_specs=in_specs,
        out_specs=pl.BlockSpec((1, ts, gw), lambda b, g, c: (b, c, g)),
        out_shape=jax.ShapeDtypeStruct((batch, seq, SSM_D_INNER), ACT_DTYPE),
        scratch_shapes=[
            pltpu.VMEM((2, SSM_CHUNK, gw + 2 * n), BF16),
            pltpu.VMEM((n, gw), F32),
        ],
        compiler_params=_params(("parallel", "parallel", "arbitrary")),
        name="ssd_scan",
    )(zx, zx, zx, zx, cw, cw, cw, cbias, cbias, cbias, a2_g, b2_g, b2_t, d_exp, ng)


def _rope_lane_order(v):
    half = ROPE_DIM // 2
    mid = LANES // 2 + half
    return jnp.concatenate([v[..., :half], v[..., ROPE_DIM:mid], v[..., half:ROPE_DIM], v[..., mid:]], axis=-1)


def _rope_tables(seq):
    half = ROPE_DIM // 2
    inv = ROPE_THETA ** (-jnp.arange(0, ROPE_DIM, 2, dtype=F32) / ROPE_DIM)
    ang = jnp.arange(seq, dtype=F32)[:, None] * inv[None, :]
    cos, sin = jnp.cos(ang), jnp.sin(ang)
    gap = LANES // 2 - half
    cosf = jnp.concatenate([cos, jnp.ones((seq, gap), F32), cos, jnp.ones((seq, gap), F32)], axis=1)
    sins = jnp.concatenate([-sin, jnp.zeros((seq, gap), F32), sin, jnp.zeros((seq, gap), F32)], axis=1)
    return cosf, sins


def _attn_layer(h, norm_g, w_qkv, layer, q_gain, k_gain, lam, subln, w_o, lam_init, tables, *, batch, seq):
    q_gain, k_gain = q_gain.astype(F32), k_gain.astype(F32)
    qkv = _norm_matmul(h, norm_g, w_qkv, layer, mode="qkv", out_dtype=ACT_DTYPE,
                       qkv_extra=(_rope_lane_order(q_gain), _rope_lane_order(k_gain)) + tables, seq=seq)
    safe = _scores_are_bounded(q_gain, k_gain)
    o = _diff_attention(qkv.reshape(batch, seq, -1), safe, lam.astype(F32), subln.astype(F32),
                        batch=batch, seq=seq, lam_init=lam_init)
    return _matmul_residual(o.reshape(batch * seq, DA_V_WIDTH), w_o, layer, h)


def _ssm_layer(h, norm_g, w_in, w_dt, layer, conv_w, conv_b, dt_bias, a_log, d_skip, norm_gain, w_out,
               *, batch, seq):
    zx = _norm_matmul(h, norm_g, w_in, layer, mode="plain", out_dtype=ACT_DTYPE, n_out=SSM_ZX_WIDTH)
    dt_raw = _norm_matmul(h, norm_g, w_dt, layer, mode="plain", out_dtype=F32)
    a2_g, b2_g, b2_t = _ssd_prep(dt_raw, dt_bias, a_log, batch=batch, seq=seq)
    y = _ssd(zx.reshape(batch, seq, SSM_ZX_WIDTH), a2_g, b2_g, b2_t, conv_w, conv_b, d_skip, norm_gain,
             batch=batch, seq=seq)
    return _matmul_residual(y.reshape(batch * seq, SSM_D_INNER), w_out, layer, h)


def _mlp_layer(h, norm_g, w1, w2, layer):
    a = _norm_matmul(h, norm_g, w1, layer, mode="relu2", out_dtype=ACT_DTYPE)
    return _matmul_residual(a, w2, layer, h)


def kernel(x, mixer_norm, mlp_norm, attn_w_qkv, attn_q_norm, attn_k_norm, attn_lambda, attn_subln, attn_w_o, ssm_w_in, ssm_conv_w, ssm_conv_b, ssm_dt_bias, ssm_a_log, ssm_d, ssm_norm, ssm_w_out, mlp_w1, mlp_w2):
    batch, seq, d_model = x.shape
    depth = mixer_norm.shape[0]
    h = x.reshape(batch * seq, d_model).astype(F32)
    tables = _rope_tables(seq)
    n_attn = attn_w_qkv.shape[0]
    qk_cols = attn_w_qkv[..., :2 * DA_QK_WIDTH].reshape(n_attn, d_model, 2 * DA_QK_WIDTH // LANES, LANES)
    w_qkv = jnp.concatenate([_rope_lane_order(qk_cols).reshape(n_attn, d_model, 2 * DA_QK_WIDTH),
                             attn_w_qkv[..., 2 * DA_QK_WIDTH:]], axis=-1).astype(BF16)
    w_o = attn_w_o.astype(BF16)
    w_in = ssm_w_in.astype(BF16)
    w_dt = jnp.pad(ssm_w_in[..., SSM_ZX_WIDTH:], ((0, 0), (0, 0), (0, LANES - SSM_HEADS))).astype(BF16)
    w_out = ssm_w_out.astype(BF16)
    w1 = mlp_w1.astype(BF16)
    w2 = mlp_w2.astype(BF16)
    for i in range(depth):
        j = i // 2
        if i % 2 == 0:
            h = _attn_layer(h, mixer_norm[i].astype(F32), w_qkv, j, attn_q_norm[j], attn_k_norm[j],
                            attn_lambda[j], attn_subln[j], w_o, _lambda_init(i), tables,
                            batch=batch, seq=seq)
        else:
            h = _ssm_layer(h, mixer_norm[i].astype(F32), w_in, w_dt, j, ssm_conv_w[j], ssm_conv_b[j],
                           ssm_dt_bias[j], ssm_a_log[j], ssm_d[j], ssm_norm[j], w_out,
                           batch=batch, seq=seq)
        h = _mlp_layer(h, mlp_norm[i].astype(F32), w1, w2, i)
    return h.reshape(batch, seq, d_model).astype(x.dtype)
```

```python
import functools
import math

import jax
import jax.numpy as jnp
from jax import lax
from jax.experimental import pallas as pl
from jax.experimental.pallas import tpu as pltpu

F32 = jnp.float32
BF16 = jnp.bfloat16

EPS = 1e-6
LANES = 128
SUBLANES = 8
VMEM_LIMIT_BYTES = 56 * 1024 * 1024

DA_HEADS = 8
DA_HEAD_DIM = 128
DA_V_DIM = 2 * DA_HEAD_DIM
DA_QK_WIDTH = DA_HEADS * 2 * DA_HEAD_DIM
DA_V_WIDTH = DA_HEADS * DA_V_DIM
ROPE_THETA = 500000.0
ROPE_DIM = DA_HEAD_DIM // 4
SSM_HEAD_DIM = 64
SSM_GROUPS = 8
SSM_HPG = 8
SSM_D_STATE = 128
SSM_CONV = 4
SSM_CHUNK = 128
SSM_GROUP_WIDTH = SSM_HPG * SSM_HEAD_DIM
SSM_D_INNER = SSM_GROUPS * SSM_GROUP_WIDTH
SSM_HEADS = SSM_GROUPS * SSM_HPG
SSM_BC_WIDTH = SSM_GROUPS * SSM_D_STATE
SSM_ZX_WIDTH = 2 * SSM_D_INNER + 2 * SSM_BC_WIDTH

LOG2_E = math.log2(math.e)
ACT_DTYPE = BF16


def _lambda_init(layer_idx):
    return 0.8 - 0.6 * math.exp(-0.3 * layer_idx)


def _sigmoid(v):
    return 1.0 / (1.0 + jnp.exp(-v))


def _softplus(v):
    return jnp.maximum(v, 0.0) + jnp.log(1.0 + jnp.exp(-jnp.abs(v)))


def _params(sem):
    return pltpu.CompilerParams(dimension_semantics=sem, vmem_limit_bytes=VMEM_LIMIT_BYTES)


def _pick(total, pref):
    t = min(total, pref)
    assert total % t == 0, (total, pref)
    return t


def _rope_group(v, gain, cosf, sina, sinb):
    ms = jnp.mean(v * v, axis=-1, keepdims=True)
    y = v * lax.rsqrt(ms + EPS) * gain
    up = pltpu.roll(y, LANES - ROPE_DIM // 2, axis=1)
    dn = pltpu.roll(y, ROPE_DIM // 2, axis=1)
    return y * cosf + up * sina + dn * sinb


def _norm_mm_kernel(x_ref, g_ref, w_ref, *rest, mode, tn, q_scale):
    if mode == "qkv":
        qg_ref, kg_ref, cos_ref, sa_ref, sb_ref, o_ref, xn_ref = rest
    else:
        o_ref, xn_ref = rest
    j = pl.program_id(1)

    @pl.when(j == 0)
    def _():
        x = x_ref[...]
        ms = jnp.mean(x * x, axis=-1, keepdims=True)
        xn_ref[...] = (x * lax.rsqrt(ms + EPS) * g_ref[...]).astype(BF16)

    acc = jnp.dot(xn_ref[...], w_ref[...], preferred_element_type=F32)
    if mode == "plain":
        o_ref[...] = acc.astype(o_ref.dtype)
    elif mode == "relu2":
        r = jnp.maximum(acc, 0.0)
        o_ref[...] = (r * r).astype(o_ref.dtype)
    else:
        n_q = DA_QK_WIDTH // tn

        @pl.when(j < 2 * n_q)
        def _():
            gain = jnp.where(j < n_q, qg_ref[...] * q_scale, kg_ref[...])
            cosf, sina, sinb = cos_ref[...], sa_ref[...], sb_ref[...]
            for g in range(tn // LANES):
                sl = slice(g * LANES, (g + 1) * LANES)
                o_ref[:, sl] = _rope_group(acc[:, sl], gain, cosf, sina, sinb).astype(o_ref.dtype)

        @pl.when(j >= 2 * n_q)
        def _():
            o_ref[...] = acc.astype(o_ref.dtype)


def _norm_matmul(x, gain, w, layer, *, mode, out_dtype, n_out=None, tm_pref=1024, tn_pref=1024,
                 qkv_extra=None, seq=None):
    t, k = x.shape
    n = w.shape[2] if n_out is None else n_out
    tm = _pick(seq if mode == "qkv" else t, tm_pref)
    tn = _pick(n, tn_pref)
    grid = (t // tm, n // tn)
    in_specs = [
        pl.BlockSpec((tm, k), lambda i, j: (i, 0)),
        pl.BlockSpec((1, k), lambda i, j: (0, 0)),
        pl.BlockSpec((None, k, tn), lambda i, j: (layer, 0, j)),
    ]
    args = [x, gain.reshape(1, k), w]
    if mode == "qkv":
        qg, kg, cosf, sina, sinb = qkv_extra
        assert seq % tm == 0 and DA_QK_WIDTH % tn == 0
        n_s = seq // tm
        vec = pl.BlockSpec((1, LANES), lambda i, j: (0, 0))
        tab = pl.BlockSpec((tm, LANES), lambda i, j: (i % n_s, 0))
        in_specs += [vec, vec, tab, tab, tab]
        args += [qg.reshape(1, LANES), kg.reshape(1, LANES), cosf, sina, sinb]
    return pl.pallas_call(
        functools.partial(_norm_mm_kernel, mode=mode, tn=tn, q_scale=DA_HEAD_DIM ** -0.5 * LOG2_E),
        grid=grid,
        in_specs=in_specs,
        out_specs=pl.BlockSpec((tm, tn), lambda i, j: (i, j)),
        out_shape=jax.ShapeDtypeStruct((t, n), out_dtype),
        scratch_shapes=[pltpu.VMEM((tm, k), BF16)],
        compiler_params=_params(("parallel", "arbitrary")),
        name="norm_mm_" + mode,
    )(*args)


def _mm_res_kernel(x_ref, w_ref, r_ref, o_ref, *scratch, nk):
    p = jnp.dot(x_ref[...], w_ref[...], preferred_element_type=F32)
    if nk == 1:
        o_ref[...] = r_ref[...] + p
        return
    acc_ref, = scratch
    k = pl.program_id(2)

    @pl.when(k == 0)
    def _():
        acc_ref[...] = p

    if nk > 2:
        @pl.when((k > 0) & (k < nk - 1))
        def _():
            acc_ref[...] += p

    @pl.when(k == nk - 1)
    def _():
        o_ref[...] = r_ref[...] + (acc_ref[...] + p)


def _matmul_residual(x, w, layer, res, *, tm_pref=1024, tn_pref=1024, tk_pref=2048):
    t, k = x.shape
    n = w.shape[2]
    tm, tn, tk = _pick(t, tm_pref), _pick(n, tn_pref), _pick(k, tk_pref)
    nk = k // tk
    return pl.pallas_call(
        functools.partial(_mm_res_kernel, nk=nk),
        grid=(t // tm, n // tn, nk),
        in_specs=[
            pl.BlockSpec((tm, tk), lambda i, j, kk: (i, kk)),
            pl.BlockSpec((None, tk, tn), lambda i, j, kk: (layer, kk, j)),
            pl.BlockSpec((tm, tn), lambda i, j, kk: (i, j)),
        ],
        out_specs=pl.BlockSpec((tm, tn), lambda i, j, kk: (i, j)),
        out_shape=jax.ShapeDtypeStruct((t, n), F32),
        scratch_shapes=[pltpu.VMEM((tm, tn), F32)] if nk > 1 else [],
        compiler_params=_params(("parallel", "parallel", "arbitrary")),
        name="mm_res",
    )(x, w, res)


NEG_BIG = -1e30
MAX_UNSHIFTED_SCORE = 64.0


def _attn_kernel(safe_ref, lam_ref, sg_ref, q_ref, k_ref, v_ref, o_ref, m_ref, l_ref, acc_ref, lp_ref,
                 *, tq, lam_init):
    qi = pl.program_id(2)
    d = DA_HEAD_DIM
    wide = 2 * tq
    safe = safe_ref[0] == 1

    acc_ref[...] = jnp.zeros(acc_ref.shape, F32)

    def scores(c, kb, width, diag_col):
        qc = q_ref[0, :, c * d:(c + 1) * d]
        s = lax.dot_general(qc, kb[:, c * d:(c + 1) * d], (((1,), (1,)), ((), ())),
                            preferred_element_type=F32)
        if diag_col is None:
            return s, None
        row = lax.broadcasted_iota(jnp.int32, (tq, width), 0)
        col = lax.broadcasted_iota(jnp.int32, (tq, width), 1)
        return s, col <= row + diag_col

    def step_unshifted(start, width, diag_col):
        kb = k_ref[0, pl.ds(start, width), :]
        vb = v_ref[0, pl.ds(start, width), :]
        for c in range(2):
            s, keep = scores(c, kb, width, diag_col)
            p = jnp.exp2(s)
            if keep is not None:
                p = jnp.where(keep, p, 0.0)
            part = p[:, 0:LANES]
            for g in range(1, width // LANES):
                part = part + p[:, g * LANES:(g + 1) * LANES]
            lp_ref[c] += part
            acc_ref[c] += jnp.dot(p.astype(BF16), vb, preferred_element_type=F32)

    def step_online(start, width, diag_col):
        kb = k_ref[0, pl.ds(start, width), :]
        vb = v_ref[0, pl.ds(start, width), :]
        for c in range(2):
            s, keep = scores(c, kb, width, diag_col)
            if keep is not None:
                s = jnp.where(keep, s, NEG_BIG)
            m_old = m_ref[c]
            m_new = jnp.maximum(m_old, jnp.max(s, axis=-1, keepdims=True))
            alpha = jnp.exp2(m_old - m_new)
            p = jnp.exp2(s - m_new)
            l_ref[c] = alpha * l_ref[c] + jnp.sum(p, axis=-1, keepdims=True)
            acc_ref[c] = alpha * acc_ref[c] + jnp.dot(p.astype(BF16), vb, preferred_element_type=F32)
            m_ref[c] = m_new

    def run(step):
        def body(j, carry):
            step(pl.multiple_of(j * wide, wide), wide, None)
            return carry

        lax.fori_loop(0, qi // 2, body, 0)

        @pl.when(qi % 2 == 1)
        def _():
            step(pl.multiple_of((qi - 1) * tq, tq), wide, tq)

        @pl.when(qi % 2 == 0)
        def _():
            step(pl.multiple_of(qi * tq, tq), tq, 0)

    @pl.when(safe)
    def _():
        lp_ref[...] = jnp.zeros(lp_ref.shape, F32)
        run(step_unshifted)
        l_ref[...] = jnp.sum(lp_ref[...], axis=-1, keepdims=True)

    @pl.when(jnp.logical_not(safe))
    def _():
        m_ref[...] = jnp.full(m_ref.shape, NEG_BIG, F32)
        l_ref[...] = jnp.zeros(l_ref.shape, F32)
        run(step_online)

    lam = lam_ref[...]
    lam_full = (jnp.exp(jnp.sum(lam[0:1] * lam[1:2], axis=-1, keepdims=True))
                - jnp.exp(jnp.sum(lam[2:3] * lam[3:4], axis=-1, keepdims=True)) + lam_init)
    o = acc_ref[0] / l_ref[0] - lam_full * (acc_ref[1] / l_ref[1])
    ms = jnp.mean(o * o, axis=-1, keepdims=True)
    o_ref[0] = (o * lax.rsqrt(ms + EPS) * (sg_ref[...] * (1.0 - lam_init))).astype(o_ref.dtype)


def _scores_are_bounded(q_gain, k_gain):
    bound = (jnp.max(jnp.abs(q_gain)) * jnp.max(jnp.abs(k_gain))
             * (DA_HEAD_DIM * DA_HEAD_DIM ** -0.5 * LOG2_E * 1.02))
    return (bound <= MAX_UNSHIFTED_SCORE).astype(jnp.int32).reshape(1)


def _diff_attention(qkv, safe, lam, subln, *, batch, seq, lam_init, tq_pref=512):
    tq = _pick(seq, tq_pref)
    hw = 2 * DA_HEAD_DIM
    kern = functools.partial(_attn_kernel, tq=tq, lam_init=lam_init)
    return pl.pallas_call(
        kern,
        grid=(batch, DA_HEADS, seq // tq),
        in_specs=[
            pl.BlockSpec(memory_space=pltpu.SMEM),
            pl.BlockSpec((4, DA_HEAD_DIM), lambda b, h, i: (0, 0)),
            pl.BlockSpec((1, DA_V_DIM), lambda b, h, i: (0, 0)),
            pl.BlockSpec((1, tq, hw), lambda b, h, i: (b, i, h)),
            pl.BlockSpec((1, seq, hw), lambda b, h, i: (b, 0, DA_HEADS + h)),
            pl.BlockSpec((1, seq, DA_V_DIM), lambda b, h, i: (b, 0, 2 * DA_HEADS + h)),
        ],
        out_specs=pl.BlockSpec((1, tq, DA_V_DIM), lambda b, h, i: (b, i, h)),
        out_shape=jax.ShapeDtypeStruct((batch, seq, DA_V_WIDTH), ACT_DTYPE),
        scratch_shapes=[
            pltpu.VMEM((2, tq, 1), F32),
            pltpu.VMEM((2, tq, 1), F32),
            pltpu.VMEM((2, tq, DA_V_DIM), F32),
            pltpu.VMEM((2, tq, LANES), F32),
        ],
        compiler_params=_params(("parallel", "parallel", "arbitrary")),
        name="diff_attn",
    )(safe, lam, subln.reshape(1, DA_V_DIM), qkv, qkv, qkv)


def _split3(v):
    hi = v.astype(BF16)
    r1 = v - hi.astype(F32)
    mid = r1.astype(BF16)
    lo = (r1 - mid.astype(F32)).astype(BF16)
    return hi, mid, lo


def _ssd_prep_kernel(dt_ref, bias_ref, alog_ref, sel_ref, a2g_ref, eaw_ref, b2t_ref, *, ts):
    L = SSM_CHUNK
    dt = _softplus(dt_ref[0] + bias_ref[...])
    dta2 = dt * (-LOG2_E * jnp.exp(alog_ref[...]))
    ld2 = jnp.log(dt) * LOG2_E
    row = lax.broadcasted_iota(jnp.int32, (L, L), 0)
    col = lax.broadcasted_iota(jnp.int32, (L, L), 1)
    tril = jnp.where(row >= col, 1.0, 0.0).astype(BF16)
    for ci in range(ts // L):
        sl = slice(ci * L, (ci + 1) * L)
        hi, mid, lo = _split3(dta2[sl])
        a2 = (jnp.dot(tril, lo, preferred_element_type=F32) + jnp.dot(tril, mid, preferred_element_type=F32)
              + jnp.dot(tril, hi, preferred_element_type=F32))
        b2 = a2 - ld2[sl]
        b2t_ref[0, :, sl] = b2.T
        ea_w = jnp.concatenate([jnp.exp2(a2).astype(BF16), jnp.exp2(a2[L - 1:L, :] - b2).astype(BF16)], axis=1)
        for g in range(SSM_GROUPS):
            a2g_ref[0, g, sl, :] = a2[:, g * SSM_HPG:(g + 1) * SSM_HPG]
            eaw_ref[0, g, sl, :] = jnp.dot(ea_w, sel_ref[g], preferred_element_type=F32).astype(BF16)


def _ssd_prep(dt_raw, dt_bias, a_log, *, batch, seq, ts_pref=512):
    ts = _pick(seq, ts_pref)
    pad = LANES - SSM_HEADS
    bias = jnp.pad(dt_bias.astype(F32), (0, pad)).reshape(1, LANES)
    alog = jnp.pad(a_log.astype(F32), (0, pad)).reshape(1, LANES)
    r = jnp.arange(2 * LANES)[None, :, None]
    j = jnp.arange(LANES)[None, None, :]
    g = jnp.arange(SSM_GROUPS)[:, None, None]
    sel = (((r == g * SSM_HPG + j) & (j < SSM_HPG))
           | ((r == LANES + g * SSM_HPG + j - SSM_HPG) & (j >= SSM_HPG) & (j < 2 * SSM_HPG))).astype(BF16)
    col = pl.BlockSpec((1, ts, LANES), lambda b, c: (b, c, 0))
    rowl = pl.BlockSpec((1, LANES, ts), lambda b, c: (b, 0, c))
    vec = pl.BlockSpec((1, LANES), lambda b, c: (0, 0))
    return pl.pallas_call(
        functools.partial(_ssd_prep_kernel, ts=ts),
        grid=(batch, seq // ts),
        in_specs=[col, vec, vec, pl.BlockSpec((SSM_GROUPS, 2 * LANES, LANES), lambda b, c: (0, 0, 0))],
        out_specs=[pl.BlockSpec((1, SSM_GROUPS, ts, SSM_HPG), lambda b, c: (b, 0, c, 0)),
                   pl.BlockSpec((1, SSM_GROUPS, ts, LANES), lambda b, c: (b, 0, c, 0)),
                   rowl],
        out_shape=[jax.ShapeDtypeStruct((batch, SSM_GROUPS, seq, SSM_HPG), F32),
                   jax.ShapeDtypeStruct((batch, SSM_GROUPS, seq, LANES), BF16),
                   jax.ShapeDtypeStruct((batch, LANES, seq), F32)],
        compiler_params=_params(("parallel", "parallel")),
        name="ssd_prep",
    )(dt_raw.reshape(batch, seq, LANES), bias, alog, sel)


def _ssd_kernel(x_ref, z_ref, b_ref, c_ref, cwx_ref, cwb_ref, cwc_ref, cbx_ref, cbb_ref, cbc_ref,
                a2g_ref, eaw_ref, b2t_ref, shift_ref, expand_ref, d_ref, ng_ref, o_ref, tail_ref, st_ref, *, ts):
    L = SSM_CHUNK
    gw, n = SSM_GROUP_WIDTH, SSM_D_STATE
    cidx = pl.program_id(2)

    slot = cidx % 2

    @pl.when(cidx == 0)
    def _():
        tail_ref[0] = jnp.zeros(tail_ref.shape[1:], tail_ref.dtype)
        st_ref[...] = jnp.zeros(st_ref.shape, F32)

    def rows(lo, hi):
        return jnp.concatenate([x_ref[0, lo:hi, :], b_ref[0, lo:hi, :], c_ref[0, lo:hi, :]], axis=1)

    row = lax.broadcasted_iota(jnp.int32, (L, L), 0)
    col = lax.broadcasted_iota(jnp.int32, (L, L), 1)
    causal = row >= col
    low_half = lax.broadcasted_iota(jnp.int32, (1, LANES), 1) < SSM_HEAD_DIM
    lane = lax.broadcasted_iota(jnp.int32, (L, LANES), 1)
    keep_lo = jnp.where(lane < SSM_HEAD_DIM, 1.0, 0.0).astype(BF16)
    keep_hi = jnp.where(lane < SSM_HEAD_DIM, 0.0, 1.0).astype(BF16)

    n_shift = SSM_CONV - 1
    conv_w = jnp.concatenate([cwx_ref[...], cwb_ref[...], cwc_ref[...]], axis=1)
    conv_b = jnp.concatenate([cbx_ref[...], cbb_ref[...], cbc_ref[...]], axis=1)

    st = st_ref[...]
    for ci in range(ts // L):
        r0 = ci * L
        win = jnp.concatenate([tail_ref[slot], rows(0, L)], axis=0) if ci == 0 else rows(r0 - L, r0 + L)
        lagged = jnp.dot(shift_ref[...], win, preferred_element_type=F32)
        acc = conv_b + conv_w[n_shift:n_shift + 1, :] * win[L:2 * L, :].astype(F32)
        for k in range(n_shift):
            acc = acc + conv_w[k:k + 1, :] * lagged[k * L:(k + 1) * L, :]
        act = acc * _sigmoid(acc)
        xc, bc, cc = act[:, 0:gw], act[:, gw:gw + n], act[:, gw + n:gw + 2 * n]
        cb = lax.dot_general(cc.astype(BF16), bc.astype(BF16), (((1,), (1,)), ((), ())),
                             preferred_element_type=F32)
        cbb = cb.astype(BF16)
        btb = bc.T.astype(BF16)
        a2c = a2g_ref[0, 0, r0:r0 + L, :]
        b2r = b2t_ref[0, :, r0:r0 + L]
        dlast = jnp.exp2(a2c[L - 1:L, :])
        spread = jnp.dot(eaw_ref[0, 0, r0:r0 + L, :], expand_ref[...], preferred_element_type=F32)
        ea_x, w_x = spread[:, 0:gw], spread[:, gw:2 * gw]
        xcb = xc.astype(BF16)
        y_off = jnp.dot(cc.astype(BF16), st.astype(BF16), preferred_element_type=F32)
        xw = (xc * w_x).astype(BF16)
        s_new = jnp.dot(btb, xw, preferred_element_type=F32)
        ys, ds = [], []
        for j in range(SSM_HPG // 2):
            xpair = xcb[:, j * LANES:(j + 1) * LANES]
            rhs = jnp.concatenate([xpair * keep_lo, xpair * keep_hi], axis=0)
            ms2 = []
            for h in (2 * j, 2 * j + 1):
                seg = a2c[:, h:h + 1] - b2r[h:h + 1, :]
                ms2.append(cbb * jnp.exp2(jnp.where(causal, seg, -jnp.inf)).astype(BF16))
            ys.append(jnp.dot(jnp.concatenate(ms2, axis=1), rhs, preferred_element_type=F32))
            ds.append(jnp.where(low_half, dlast[:, 2 * j:2 * j + 1], dlast[:, 2 * j + 1:2 * j + 2]))
        st = st * jnp.concatenate(ds, axis=1) + s_new
        y = jnp.concatenate(ys, axis=1) + y_off * ea_x + d_ref[...] * xc
        zz = z_ref[0, r0:r0 + L, :].astype(F32)
        y = y * (zz * _sigmoid(zz))
        ms = jnp.mean(y * y, axis=-1, keepdims=True)
        o_ref[0, r0:r0 + L, :] = (y * lax.rsqrt(ms + EPS) * ng_ref[...]).astype(o_ref.dtype)

    st_ref[...] = st
    tail_ref[1 - slot] = rows(ts - L, ts)


def _ssd(zx, a2_g, eaw_g, b2_t, conv_w, conv_b, d_skip, norm_gain, *, batch, seq, ts_pref=1024):
    assert zx.dtype == BF16, "the shift-matrix conv is exact only for bf16 inputs"
    ts = _pick(seq, ts_pref)
    gw, n, G = SSM_GROUP_WIDTH, SSM_D_STATE, SSM_GROUPS
    L, n_shift = SSM_CHUNK, SSM_CONV - 1
    srow = jnp.arange(n_shift * L)[:, None]
    shift = (jnp.arange(2 * L)[None, :] == (srow % L) + (L - n_shift) + srow // L).astype(BF16)
    er = jnp.arange(LANES)[:, None]
    ec = jnp.arange(2 * gw)[None, :]
    expand = (er == (ec % gw) // SSM_HEAD_DIM + SSM_HPG * (ec // gw)).astype(BF16)
    x_blk0 = SSM_D_INNER // gw
    b_blk0 = 2 * SSM_D_INNER // n
    c_blk0 = b_blk0 + SSM_BC_WIDTH // n
    cw = conv_w.astype(F32)
    cbias = conv_b.astype(F32).reshape(1, -1)
    d_exp = jnp.repeat(d_skip.astype(F32), SSM_HEAD_DIM).reshape(1, SSM_D_INNER)
    ng = norm_gain.astype(F32).reshape(1, SSM_D_INNER)
    cwb0 = SSM_D_INNER // n
    cwc0 = cwb0 + SSM_BC_WIDTH // n
    in_specs = [
        pl.BlockSpec((1, ts, gw), lambda b, g, c: (b, c, x_blk0 + g)),
        pl.BlockSpec((1, ts, gw), lambda b, g, c: (b, c, g)),
        pl.BlockSpec((1, ts, n), lambda b, g, c: (b, c, b_blk0 + g)),
        pl.BlockSpec((1, ts, n), lambda b, g, c: (b, c, c_blk0 + g)),
        pl.BlockSpec((SSM_CONV, gw), lambda b, g, c: (0, g)),
        pl.BlockSpec((SSM_CONV, n), lambda b, g, c: (0, cwb0 + g)),
        pl.BlockSpec((SSM_CONV, n), lambda b, g, c: (0, cwc0 + g)),
        pl.BlockSpec((1, gw), lambda b, g, c: (0, g)),
        pl.BlockSpec((1, n), lambda b, g, c: (0, cwb0 + g)),
        pl.BlockSpec((1, n), lambda b, g, c: (0, cwc0 + g)),
        pl.BlockSpec((1, 1, ts, SSM_HPG), lambda b, g, c: (b, g, c, 0)),
        pl.BlockSpec((1, 1, ts, LANES), lambda b, g, c: (b, g, c, 0)),
        pl.BlockSpec((1, SSM_HPG, ts), lambda b, g, c: (b, g, c)),
        pl.BlockSpec((n_shift * L, 2 * L), lambda b, g, c: (0, 0)),
        pl.BlockSpec((LANES, 2 * gw), lambda b, g, c: (0, 0)),
        pl.BlockSpec((1, gw), lambda b, g, c: (0, g)),
        pl.BlockSpec((1, gw), lambda b, g, c: (0, g)),
    ]
    return pl.pallas_call(
        functools.partial(_ssd_kernel, ts=ts),
        grid=(batch, G, seq // ts),
        in_specs=in_specs,
        out_specs=pl.BlockSpec((1, ts, gw), lambda b, g, c: (b, c, g)),
        out_shape=jax.ShapeDtypeStruct((batch, seq, SSM_D_INNER), ACT_DTYPE),
        scratch_shapes=[
            pltpu.VMEM((2, SSM_CHUNK, gw + 2 * n), BF16),
            pltpu.VMEM((n, gw), F32),
        ],
        compiler_params=_params(("parallel", "parallel", "arbitrary")),
        name="ssd_scan",
    )(zx, zx, zx, zx, cw, cw, cw, cbias, cbias, cbias, a2_g, eaw_g, b2_t, shift, expand, d_exp, ng)


def _rope_tables(seq):
    half = ROPE_DIM // 2
    inv = ROPE_THETA ** (-jnp.arange(0, ROPE_DIM, 2, dtype=F32) / ROPE_DIM)
    ang = jnp.arange(seq, dtype=F32)[:, None] * inv[None, :]
    cos, sin = jnp.cos(ang), jnp.sin(ang)
    cosf = jnp.concatenate([cos, cos, jnp.ones((seq, LANES - ROPE_DIM), F32)], axis=1)
    sina = jnp.concatenate([-sin, jnp.zeros((seq, LANES - half), F32)], axis=1)
    sinb = jnp.concatenate([jnp.zeros((seq, half), F32), sin, jnp.zeros((seq, LANES - ROPE_DIM), F32)], axis=1)
    return cosf, sina, sinb


def _attn_layer(h, norm_g, w_qkv, layer, q_gain, k_gain, lam, subln, w_o, lam_init, tables, *, batch, seq):
    q_gain, k_gain = q_gain.astype(F32), k_gain.astype(F32)
    qkv = _norm_matmul(h, norm_g, w_qkv, layer, mode="qkv", out_dtype=ACT_DTYPE,
                       qkv_extra=(q_gain, k_gain) + tables, seq=seq)
    safe = _scores_are_bounded(q_gain, k_gain)
    o = _diff_attention(qkv.reshape(batch, seq, -1), safe, lam.astype(F32), subln.astype(F32),
                        batch=batch, seq=seq, lam_init=lam_init)
    return _matmul_residual(o.reshape(batch * seq, DA_V_WIDTH), w_o, layer, h)


def _ssm_layer(h, norm_g, w_in, w_dt, layer, conv_w, conv_b, dt_bias, a_log, d_skip, norm_gain, w_out,
               *, batch, seq):
    zx = _norm_matmul(h, norm_g, w_in, layer, mode="plain", out_dtype=ACT_DTYPE, n_out=SSM_ZX_WIDTH)
    dt_raw = _norm_matmul(h, norm_g, w_dt, layer, mode="plain", out_dtype=F32)
    a2_g, eaw_g, b2_t = _ssd_prep(dt_raw, dt_bias, a_log, batch=batch, seq=seq)
    y = _ssd(zx.reshape(batch, seq, SSM_ZX_WIDTH), a2_g, eaw_g, b2_t, conv_w, conv_b, d_skip, norm_gain,
             batch=batch, seq=seq)
    return _matmul_residual(y.reshape(batch * seq, SSM_D_INNER), w_out, layer, h)


def _mlp_layer(h, norm_g, w1, w2, layer):
    a = _norm_matmul(h, norm_g, w1, layer, mode="relu2", out_dtype=ACT_DTYPE)
    return _matmul_residual(a, w2, layer, h)


def kernel(x, mixer_norm, mlp_norm, attn_w_qkv, attn_q_norm, attn_k_norm, attn_lambda, attn_subln, attn_w_o, ssm_w_in, ssm_conv_w, ssm_conv_b, ssm_dt_bias, ssm_a_log, ssm_d, ssm_norm, ssm_w_out, mlp_w1, mlp_w2):
    batch, seq, d_model = x.shape
    depth = mixer_norm.shape[0]
    h = x.reshape(batch * seq, d_model).astype(F32)
    tables = _rope_tables(seq)
    w_qkv = attn_w_qkv.astype(BF16)
    w_o = attn_w_o.astype(BF16)
    w_in = ssm_w_in.astype(BF16)
    w_dt = jnp.pad(ssm_w_in[..., SSM_ZX_WIDTH:], ((0, 0), (0, 0), (0, LANES - SSM_HEADS))).astype(BF16)
    w_out = ssm_w_out.astype(BF16)
    w1 = mlp_w1.astype(BF16)
    w2 = mlp_w2.astype(BF16)
    for i in range(depth):
        j = i // 2
        if i % 2 == 0:
            h = _attn_layer(h, mixer_norm[i].astype(F32), w_qkv, j, attn_q_norm[j], attn_k_norm[j],
                            attn_lambda[j], attn_subln[j], w_o, _lambda_init(i), tables,
                            batch=batch, seq=seq)
        else:
            h = _ssm_layer(h, mixer_norm[i].astype(F32), w_in, w_dt, j, ssm_conv_w[j], ssm_conv_b[j],
                           ssm_dt_bias[j], ssm_a_log[j], ssm_d[j], ssm_norm[j], w_out,
                           batch=batch, seq=seq)
        h = _mlp_layer(h, mlp_norm[i].astype(F32), w1, w2, i)
    return h.reshape(batch, seq, d_model).astype(x.dtype)
```

```python
import functools
import math

import jax
import jax.numpy as jnp
from jax import lax
from jax.experimental import pallas as pl
from jax.experimental.pallas import tpu as pltpu

F32 = jnp.float32
BF16 = jnp.bfloat16

EPS = 1e-6
LANES = 128
SUBLANES = 8
VMEM_LIMIT_BYTES = 56 * 1024 * 1024

DA_HEADS = 8
DA_HEAD_DIM = 128
DA_V_DIM = 2 * DA_HEAD_DIM
DA_QK_WIDTH = DA_HEADS * 2 * DA_HEAD_DIM
DA_V_WIDTH = DA_HEADS * DA_V_DIM
ROPE_THETA = 500000.0
ROPE_DIM = DA_HEAD_DIM // 4
SSM_HEAD_DIM = 64
SSM_GROUPS = 8
SSM_HPG = 8
SSM_D_STATE = 128
SSM_CONV = 4
SSM_CHUNK = 128
SSM_GROUP_WIDTH = SSM_HPG * SSM_HEAD_DIM
SSM_D_INNER = SSM_GROUPS * SSM_GROUP_WIDTH
SSM_HEADS = SSM_GROUPS * SSM_HPG
SSM_BC_WIDTH = SSM_GROUPS * SSM_D_STATE
SSM_ZX_WIDTH = 2 * SSM_D_INNER + 2 * SSM_BC_WIDTH

LOG2_E = math.log2(math.e)
ACT_DTYPE = BF16


def _lambda_init(layer_idx):
    return 0.8 - 0.6 * math.exp(-0.3 * layer_idx)


def _sigmoid(v):
    return 1.0 / (1.0 + jnp.exp(-v))


def _softplus(v):
    return jnp.maximum(v, 0.0) + jnp.log(1.0 + jnp.exp(-jnp.abs(v)))


def _params(sem):
    return pltpu.CompilerParams(dimension_semantics=sem, vmem_limit_bytes=VMEM_LIMIT_BYTES)


def _pick(total, pref):
    t = min(total, pref)
    assert total % t == 0, (total, pref)
    return t


def _rope_partner(v):
    half = ROPE_DIM // 2
    return jnp.concatenate([v[..., half:ROPE_DIM], v[..., :half], v[..., ROPE_DIM:]], axis=-1)


def _rope_group(v, mix, cos_gain, sin_gain):
    res = jnp.dot(jnp.concatenate([v.astype(BF16), (v * v).astype(BF16)], axis=1), mix,
                  preferred_element_type=F32)
    inv = lax.rsqrt(res[:, LANES:] * (1.0 / LANES) + EPS)
    return inv * (v * cos_gain + res[:, :LANES] * sin_gain)


def _norm_mm_kernel(x_ref, g_ref, w_ref, *rest, mode, tn, q_scale):
    if mode == "qkv":
        qg_ref, kg_ref, qgp_ref, kgp_ref, cos_ref, sin_ref, mix_ref, o_ref, xn_ref = rest
    else:
        o_ref, xn_ref = rest
    j = pl.program_id(1)

    @pl.when(j == 0)
    def _():
        x = x_ref[...]
        ms = jnp.mean(x * x, axis=-1, keepdims=True)
        xn_ref[...] = (x * lax.rsqrt(ms + EPS) * g_ref[...]).astype(BF16)

    acc = jnp.dot(xn_ref[...], w_ref[...], preferred_element_type=F32)
    if mode == "plain":
        o_ref[...] = acc.astype(o_ref.dtype)
    elif mode == "relu2":
        r = jnp.maximum(acc, 0.0)
        o_ref[...] = (r * r).astype(o_ref.dtype)
    else:
        n_q = DA_QK_WIDTH // tn

        @pl.when(j < 2 * n_q)
        def _():
            cos_gain = cos_ref[...] * jnp.where(j < n_q, qg_ref[...] * q_scale, kg_ref[...])
            sin_gain = sin_ref[...] * jnp.where(j < n_q, qgp_ref[...] * q_scale, kgp_ref[...])
            mix = mix_ref[...]
            for g in range(tn // LANES):
                sl = slice(g * LANES, (g + 1) * LANES)
                o_ref[:, sl] = _rope_group(acc[:, sl], mix, cos_gain, sin_gain).astype(o_ref.dtype)

        @pl.when(j >= 2 * n_q)
        def _():
            o_ref[...] = acc.astype(o_ref.dtype)


def _norm_matmul(x, gain, w, layer, *, mode, out_dtype, n_out=None, tm_pref=1024, tn_pref=1024,
                 qkv_extra=None, seq=None):
    t, k = x.shape
    n = w.shape[2] if n_out is None else n_out
    tm = _pick(seq if mode == "qkv" else t, tm_pref)
    tn = _pick(n, tn_pref)
    grid = (t // tm, n // tn)
    in_specs = [
        pl.BlockSpec((tm, k), lambda i, j: (i, 0)),
        pl.BlockSpec((1, k), lambda i, j: (0, 0)),
        pl.BlockSpec((None, k, tn), lambda i, j: (layer, 0, j)),
    ]
    args = [x, gain.reshape(1, k), w]
    if mode == "qkv":
        qg, kg, cosf, sins = qkv_extra
        assert seq % tm == 0 and DA_QK_WIDTH % tn == 0
        n_s = seq // tm
        lane = jnp.arange(LANES)
        perm = (lane[:, None] == _rope_partner(lane)[None, :]) & (lane[None, :] < ROPE_DIM)
        zeros = jnp.zeros((LANES, LANES), F32)
        mix = jnp.block([[perm.astype(F32), zeros], [zeros, jnp.ones((LANES, LANES), F32)]]).astype(BF16)
        vec = pl.BlockSpec((1, LANES), lambda i, j: (0, 0))
        tab = pl.BlockSpec((tm, LANES), lambda i, j: (i % n_s, 0))
        in_specs += [vec, vec, vec, vec, tab, tab, pl.BlockSpec((2 * LANES, 2 * LANES), lambda i, j: (0, 0))]
        args += [qg.reshape(1, LANES), kg.reshape(1, LANES), _rope_partner(qg).reshape(1, LANES),
                 _rope_partner(kg).reshape(1, LANES), cosf, sins, mix]
    return pl.pallas_call(
        functools.partial(_norm_mm_kernel, mode=mode, tn=tn, q_scale=DA_HEAD_DIM ** -0.5 * LOG2_E),
        grid=grid,
        in_specs=in_specs,
        out_specs=pl.BlockSpec((tm, tn), lambda i, j: (i, j)),
        out_shape=jax.ShapeDtypeStruct((t, n), out_dtype),
        scratch_shapes=[pltpu.VMEM((tm, k), BF16)],
        compiler_params=_params(("parallel", "arbitrary")),
        name="norm_mm_" + mode,
    )(*args)


def _mm_res_kernel(x_ref, w_ref, r_ref, o_ref, *scratch, nk):
    p = jnp.dot(x_ref[...], w_ref[...], preferred_element_type=F32)
    if nk == 1:
        o_ref[...] = r_ref[...] + p
        return
    acc_ref, = scratch
    k = pl.program_id(2)

    @pl.when(k == 0)
    def _():
        acc_ref[...] = p

    if nk > 2:
        @pl.when((k > 0) & (k < nk - 1))
        def _():
            acc_ref[...] += p

    @pl.when(k == nk - 1)
    def _():
        o_ref[...] = r_ref[...] + (acc_ref[...] + p)


def _matmul_residual(x, w, layer, res, *, tm_pref=1024, tn_pref=1024, tk_pref=2048):
    t, k = x.shape
    n = w.shape[2]
    tm, tn, tk = _pick(t, tm_pref), _pick(n, tn_pref), _pick(k, tk_pref)
    nk = k // tk
    return pl.pallas_call(
        functools.partial(_mm_res_kernel, nk=nk),
        grid=(t // tm, n // tn, nk),
        in_specs=[
            pl.BlockSpec((tm, tk), lambda i, j, kk: (i, kk)),
            pl.BlockSpec((None, tk, tn), lambda i, j, kk: (layer, kk, j)),
            pl.BlockSpec((tm, tn), lambda i, j, kk: (i, j)),
        ],
        out_specs=pl.BlockSpec((tm, tn), lambda i, j, kk: (i, j)),
        out_shape=jax.ShapeDtypeStruct((t, n), F32),
        scratch_shapes=[pltpu.VMEM((tm, tn), F32)] if nk > 1 else [],
        compiler_params=_params(("parallel", "parallel", "arbitrary")),
        name="mm_res",
    )(x, w, res)


NEG_BIG = -1e30
MAX_UNSHIFTED_SCORE = 64.0


def _attn_kernel(safe_ref, lam_ref, sg_ref, q_ref, k_ref, v_ref, o_ref, m_ref, l_ref, acc_ref, lp_ref,
                 *, tq, lam_init):
    qi = pl.program_id(2)
    d = DA_HEAD_DIM
    wide = 2 * tq
    safe = safe_ref[0] == 1

    acc_ref[...] = jnp.zeros(acc_ref.shape, F32)

    def scores(c, kb, width, diag_col):
        qc = q_ref[0, :, c * d:(c + 1) * d]
        s = lax.dot_general(qc, kb[:, c * d:(c + 1) * d], (((1,), (1,)), ((), ())),
                            preferred_element_type=F32)
        if diag_col is None:
            return s, None
        row = lax.broadcasted_iota(jnp.int32, (tq, width), 0)
        col = lax.broadcasted_iota(jnp.int32, (tq, width), 1)
        return s, col <= row + diag_col

    def step_unshifted(start, width, diag_col):
        kb = k_ref[0, pl.ds(start, width), :]
        vb = v_ref[0, pl.ds(start, width), :]
        for c in range(2):
            s, keep = scores(c, kb, width, diag_col)
            p = jnp.exp2(s)
            if keep is not None:
                p = jnp.where(keep, p, 0.0)
            part = p[:, 0:LANES]
            for g in range(1, width // LANES):
                part = part + p[:, g * LANES:(g + 1) * LANES]
            lp_ref[c] += part
            acc_ref[c] += jnp.dot(p.astype(BF16), vb, preferred_element_type=F32)

    def step_online(start, width, diag_col):
        kb = k_ref[0, pl.ds(start, width), :]
        vb = v_ref[0, pl.ds(start, width), :]
        for c in range(2):
            s, keep = scores(c, kb, width, diag_col)
            if keep is not None:
                s = jnp.where(keep, s, NEG_BIG)
            m_old = m_ref[c]
            m_new = jnp.maximum(m_old, jnp.max(s, axis=-1, keepdims=True))
            alpha = jnp.exp2(m_old - m_new)
            p = jnp.exp2(s - m_new)
            l_ref[c] = alpha * l_ref[c] + jnp.sum(p, axis=-1, keepdims=True)
            acc_ref[c] = alpha * acc_ref[c] + jnp.dot(p.astype(BF16), vb, preferred_element_type=F32)
            m_ref[c] = m_new

    def run(step):
        def body(j, carry):
            step(pl.multiple_of(j * wide, wide), wide, None)
            return carry

        lax.fori_loop(0, qi // 2, body, 0)

        @pl.when(qi % 2 == 1)
        def _():
            step(pl.multiple_of((qi - 1) * tq, tq), wide, tq)

        @pl.when(qi % 2 == 0)
        def _():
            step(pl.multiple_of(qi * tq, tq), tq, 0)

    @pl.when(safe)
    def _():
        lp_ref[...] = jnp.zeros(lp_ref.shape, F32)
        run(step_unshifted)
        l_ref[...] = jnp.sum(lp_ref[...], axis=-1, keepdims=True)

    @pl.when(jnp.logical_not(safe))
    def _():
        m_ref[...] = jnp.full(m_ref.shape, NEG_BIG, F32)
        l_ref[...] = jnp.zeros(l_ref.shape, F32)
        run(step_online)

    lam = lam_ref[...]
    lam_full = (jnp.exp(jnp.sum(lam[0:1] * lam[1:2], axis=-1, keepdims=True))
                - jnp.exp(jnp.sum(lam[2:3] * lam[3:4], axis=-1, keepdims=True)) + lam_init)
    o = acc_ref[0] / l_ref[0] - lam_full * (acc_ref[1] / l_ref[1])
    ms = jnp.mean(o * o, axis=-1, keepdims=True)
    o_ref[0] = (o * lax.rsqrt(ms + EPS) * (sg_ref[...] * (1.0 - lam_init))).astype(o_ref.dtype)


def _scores_are_bounded(q_gain, k_gain):
    bound = (jnp.max(jnp.abs(q_gain)) * jnp.max(jnp.abs(k_gain))
             * (DA_HEAD_DIM * DA_HEAD_DIM ** -0.5 * LOG2_E * 1.02))
    return (bound <= MAX_UNSHIFTED_SCORE).astype(jnp.int32).reshape(1)


def _diff_attention(qkv, safe, lam, subln, *, batch, seq, lam_init, tq_pref=512):
    tq = _pick(seq, tq_pref)
    hw = 2 * DA_HEAD_DIM
    kern = functools.partial(_attn_kernel, tq=tq, lam_init=lam_init)
    return pl.pallas_call(
        kern,
        grid=(batch, DA_HEADS, seq // tq),
        in_specs=[
            pl.BlockSpec(memory_space=pltpu.SMEM),
            pl.BlockSpec((4, DA_HEAD_DIM), lambda b, h, i: (0, 0)),
            pl.BlockSpec((1, DA_V_DIM), lambda b, h, i: (0, 0)),
            pl.BlockSpec((1, tq, hw), lambda b, h, i: (b, i, h)),
            pl.BlockSpec((1, seq, hw), lambda b, h, i: (b, 0, DA_HEADS + h)),
            pl.BlockSpec((1, seq, DA_V_DIM), lambda b, h, i: (b, 0, 2 * DA_HEADS + h)),
        ],
        out_specs=pl.BlockSpec((1, tq, DA_V_DIM), lambda b, h, i: (b, i, h)),
        out_shape=jax.ShapeDtypeStruct((batch, seq, DA_V_WIDTH), ACT_DTYPE),
        scratch_shapes=[
            pltpu.VMEM((2, tq, 1), F32),
            pltpu.VMEM((2, tq, 1), F32),
            pltpu.VMEM((2, tq, DA_V_DIM), F32),
            pltpu.VMEM((2, tq, LANES), F32),
        ],
        compiler_params=_params(("parallel", "parallel", "arbitrary")),
        name="diff_attn",
    )(safe, lam, subln.reshape(1, DA_V_DIM), qkv, qkv, qkv)


def _split3(v):
    hi = v.astype(BF16)
    r1 = v - hi.astype(F32)
    mid = r1.astype(BF16)
    lo = (r1 - mid.astype(F32)).astype(BF16)
    return hi, mid, lo


def _ssd_prep_kernel(dt_ref, bias_ref, alog_ref, sel_ref, a2g_ref, eaw_ref, b2t_ref, *, ts):
    L = SSM_CHUNK
    dt = _softplus(dt_ref[0] + bias_ref[...])
    dta2 = dt * (-LOG2_E * jnp.exp(alog_ref[...]))
    ld2 = jnp.log(dt) * LOG2_E
    row = lax.broadcasted_iota(jnp.int32, (L, L), 0)
    col = lax.broadcasted_iota(jnp.int32, (L, L), 1)
    tril = jnp.where(row >= col, 1.0, 0.0).astype(BF16)
    for ci in range(ts // L):
        sl = slice(ci * L, (ci + 1) * L)
        hi, mid, lo = _split3(dta2[sl])
        a2 = (jnp.dot(tril, lo, preferred_element_type=F32) + jnp.dot(tril, mid, preferred_element_type=F32)
              + jnp.dot(tril, hi, preferred_element_type=F32))
        b2 = a2 - ld2[sl]
        b2t_ref[0, :, sl] = b2.T
        ea_w = jnp.concatenate([jnp.exp2(a2).astype(BF16), jnp.exp2(a2[L - 1:L, :] - b2).astype(BF16)], axis=1)
        for g in range(SSM_GROUPS):
            a2g_ref[0, g, sl, :] = a2[:, g * SSM_HPG:(g + 1) * SSM_HPG]
            eaw_ref[0, g, sl, :] = jnp.dot(ea_w, sel_ref[g], preferred_element_type=F32).astype(BF16)


def _ssd_prep(dt_raw, dt_bias, a_log, *, batch, seq, ts_pref=512):
    ts = _pick(seq, ts_pref)
    pad = LANES - SSM_HEADS
    bias = jnp.pad(dt_bias.astype(F32), (0, pad)).reshape(1, LANES)
    alog = jnp.pad(a_log.astype(F32), (0, pad)).reshape(1, LANES)
    r = jnp.arange(2 * LANES)[None, :, None]
    j = jnp.arange(LANES)[None, None, :]
    g = jnp.arange(SSM_GROUPS)[:, None, None]
    sel = (((r == g * SSM_HPG + j) & (j < SSM_HPG))
           | ((r == LANES + g * SSM_HPG + j - SSM_HPG) & (j >= SSM_HPG) & (j < 2 * SSM_HPG))).astype(BF16)
    col = pl.BlockSpec((1, ts, LANES), lambda b, c: (b, c, 0))
    rowl = pl.BlockSpec((1, LANES, ts), lambda b, c: (b, 0, c))
    vec = pl.BlockSpec((1, LANES), lambda b, c: (0, 0))
    return pl.pallas_call(
        functools.partial(_ssd_prep_kernel, ts=ts),
        grid=(batch, seq // ts),
        in_specs=[col, vec, vec, pl.BlockSpec((SSM_GROUPS, 2 * LANES, LANES), lambda b, c: (0, 0, 0))],
        out_specs=[pl.BlockSpec((1, SSM_GROUPS, ts, SSM_HPG), lambda b, c: (b, 0, c, 0)),
                   pl.BlockSpec((1, SSM_GROUPS, ts, LANES), lambda b, c: (b, 0, c, 0)),
                   rowl],
        out_shape=[jax.ShapeDtypeStruct((batch, SSM_GROUPS, seq, SSM_HPG), F32),
                   jax.ShapeDtypeStruct((batch, SSM_GROUPS, seq, LANES), BF16),
                   jax.ShapeDtypeStruct((batch, LANES, seq), F32)],
        compiler_params=_params(("parallel", "parallel")),
        name="ssd_prep",
    )(dt_raw.reshape(batch, seq, LANES), bias, alog, sel)


def _ssd_kernel(x_ref, z_ref, b_ref, c_ref, cwx_ref, cwb_ref, cwc_ref, cbx_ref, cbb_ref, cbc_ref,
                a2g_ref, eaw_ref, b2t_ref, shift_ref, expand_ref, d_ref, ng_ref, o_ref, tail_ref, st_ref, *, ts):
    L = SSM_CHUNK
    gw, n = SSM_GROUP_WIDTH, SSM_D_STATE
    cidx = pl.program_id(2)

    slot = cidx % 2

    @pl.when(cidx == 0)
    def _():
        tail_ref[0] = jnp.zeros(tail_ref.shape[1:], tail_ref.dtype)
        st_ref[...] = jnp.zeros(st_ref.shape, F32)

    def rows(lo, hi):
        return jnp.concatenate([x_ref[0, lo:hi, :], b_ref[0, lo:hi, :], c_ref[0, lo:hi, :]], axis=1)

    row = lax.broadcasted_iota(jnp.int32, (L, L), 0)
    col = lax.broadcasted_iota(jnp.int32, (L, L), 1)
    causal = row >= col
    low_half = lax.broadcasted_iota(jnp.int32, (1, LANES), 1) < SSM_HEAD_DIM
    lane = lax.broadcasted_iota(jnp.int32, (L, LANES), 1)
    keep_lo = jnp.where(lane < SSM_HEAD_DIM, 1.0, 0.0).astype(BF16)
    keep_hi = jnp.where(lane < SSM_HEAD_DIM, 0.0, 1.0).astype(BF16)

    n_shift = SSM_CONV - 1
    conv_w = jnp.concatenate([cwx_ref[...], cwb_ref[...], cwc_ref[...]], axis=1)
    conv_b = jnp.concatenate([cbx_ref[...], cbb_ref[...], cbc_ref[...]], axis=1)

    st = st_ref[...]
    for ci in range(ts // L):
        r0 = ci * L
        win = jnp.concatenate([tail_ref[slot], rows(0, L)], axis=0) if ci == 0 else rows(r0 - L, r0 + L)
        lagged = jnp.dot(shift_ref[...], win, preferred_element_type=F32)
        acc = conv_b + conv_w[n_shift:n_shift + 1, :] * win[L:2 * L, :].astype(F32)
        for k in range(n_shift):
            acc = acc + conv_w[k:k + 1, :] * lagged[k * L:(k + 1) * L, :]
        act = acc * _sigmoid(acc)
        xc, bc, cc = act[:, 0:gw], act[:, gw:gw + n], act[:, gw + n:gw + 2 * n]
        cb = lax.dot_general(cc.astype(BF16), bc.astype(BF16), (((1,), (1,)), ((), ())),
                             preferred_element_type=F32)
        cbb = cb.astype(BF16)
        btb = bc.T.astype(BF16)
        a2c = a2g_ref[0, 0, r0:r0 + L, :]
        b2r = b2t_ref[0, :, r0:r0 + L]
        dlast = jnp.exp2(a2c[L - 1:L, :])
        spread = jnp.dot(eaw_ref[0, 0, r0:r0 + L, :], expand_ref[...], preferred_element_type=F32)
        ea_x, w_x = spread[:, 0:gw], spread[:, gw:2 * gw]
        xcb = xc.astype(BF16)
        y_off = jnp.dot(cc.astype(BF16), st.astype(BF16), preferred_element_type=F32)
        xw = (xc * w_x).astype(BF16)
        s_new = jnp.dot(btb, xw, preferred_element_type=F32)
        ys, ds = [], []
        for j in range(SSM_HPG // 2):
            xpair = xcb[:, j * LANES:(j + 1) * LANES]
            rhs = jnp.concatenate([xpair * keep_lo, xpair * keep_hi], axis=0)
            ms2 = []
            for h in (2 * j, 2 * j + 1):
                seg = a2c[:, h:h + 1] - b2r[h:h + 1, :]
                ms2.append(cbb * jnp.exp2(jnp.where(causal, seg, -jnp.inf)).astype(BF16))
            ys.append(jnp.dot(jnp.concatenate(ms2, axis=1), rhs, preferred_element_type=F32))
            ds.append(jnp.where(low_half, dlast[:, 2 * j:2 * j + 1], dlast[:, 2 * j + 1:2 * j + 2]))
        st = st * jnp.concatenate(ds, axis=1) + s_new
        y = jnp.concatenate(ys, axis=1) + y_off * ea_x + d_ref[...] * xc
        zz = z_ref[0, r0:r0 + L, :].astype(F32)
        y = y * (zz * _sigmoid(zz))
        ms = jnp.mean(y * y, axis=-1, keepdims=True)
        o_ref[0, r0:r0 + L, :] = (y * lax.rsqrt(ms + EPS) * ng_ref[...]).astype(o_ref.dtype)

    st_ref[...] = st
    tail_ref[1 - slot] = rows(ts - L, ts)


def _ssd(zx, a2_g, eaw_g, b2_t, conv_w, conv_b, d_skip, norm_gain, *, batch, seq, ts_pref=1024):
    assert zx.dtype == BF16, "the shift-matrix conv is exact only for bf16 inputs"
    ts = _pick(seq, ts_pref)
    gw, n, G = SSM_GROUP_WIDTH, SSM_D_STATE, SSM_GROUPS
    L, n_shift = SSM_CHUNK, SSM_CONV - 1
    srow = jnp.arange(n_shift * L)[:, None]
    shift = (jnp.arange(2 * L)[None, :] == (srow % L) + (L - n_shift) + srow // L).astype(BF16)
    er = jnp.arange(LANES)[:, None]
    ec = jnp.arange(2 * gw)[None, :]
    expand = (er == (ec % gw) // SSM_HEAD_DIM + SSM_HPG * (ec // gw)).astype(BF16)
    x_blk0 = SSM_D_INNER // gw
    b_blk0 = 2 * SSM_D_INNER // n
    c_blk0 = b_blk0 + SSM_BC_WIDTH // n
    cw = conv_w.astype(F32)
    cbias = conv_b.astype(F32).reshape(1, -1)
    d_exp = jnp.repeat(d_skip.astype(F32), SSM_HEAD_DIM).reshape(1, SSM_D_INNER)
    ng = norm_gain.astype(F32).reshape(1, SSM_D_INNER)
    cwb0 = SSM_D_INNER // n
    cwc0 = cwb0 + SSM_BC_WIDTH // n
    in_specs = [
        pl.BlockSpec((1, ts, gw), lambda b, g, c: (b, c, x_blk0 + g)),
        pl.BlockSpec((1, ts, gw), lambda b, g, c: (b, c, g)),
        pl.BlockSpec((1, ts, n), lambda b, g, c: (b, c, b_blk0 + g)),
        pl.BlockSpec((1, ts, n), lambda b, g, c: (b, c, c_blk0 + g)),
        pl.BlockSpec((SSM_CONV, gw), lambda b, g, c: (0, g)),
        pl.BlockSpec((SSM_CONV, n), lambda b, g, c: (0, cwb0 + g)),
        pl.BlockSpec((SSM_CONV, n), lambda b, g, c: (0, cwc0 + g)),
        pl.BlockSpec((1, gw), lambda b, g, c: (0, g)),
        pl.BlockSpec((1, n), lambda b, g, c: (0, cwb0 + g)),
        pl.BlockSpec((1, n), lambda b, g, c: (0, cwc0 + g)),
        pl.BlockSpec((1, 1, ts, SSM_HPG), lambda b, g, c: (b, g, c, 0)),
        pl.BlockSpec((1, 1, ts, LANES), lambda b, g, c: (b, g, c, 0)),
        pl.BlockSpec((1, SSM_HPG, ts), lambda b, g, c: (b, g, c)),
        pl.BlockSpec((n_shift * L, 2 * L), lambda b, g, c: (0, 0)),
        pl.BlockSpec((LANES, 2 * gw), lambda b, g, c: (0, 0)),
        pl.BlockSpec((1, gw), lambda b, g, c: (0, g)),
        pl.BlockSpec((1, gw), lambda b, g, c: (0, g)),
    ]
    return pl.pallas_call(
        functools.partial(_ssd_kernel, ts=ts),
        grid=(batch, G, seq // ts),
        in_specs=in_specs,
        out_specs=pl.BlockSpec((1, ts, gw), lambda b, g, c: (b, c, g)),
        out_shape=jax.ShapeDtypeStruct((batch, seq, SSM_D_INNER), ACT_DTYPE),
        scratch_shapes=[
            pltpu.VMEM((2, SSM_CHUNK, gw + 2 * n), BF16),
            pltpu.VMEM((n, gw), F32),
        ],
        compiler_params=_params(("parallel", "parallel", "arbitrary")),
        name="ssd_scan",
    )(zx, zx, zx, zx, cw, cw, cw, cbias, cbias, cbias, a2_g, eaw_g, b2_t, shift, expand, d_exp, ng)


def _rope_tables(seq):
    inv = ROPE_THETA ** (-jnp.arange(0, ROPE_DIM, 2, dtype=F32) / ROPE_DIM)
    ang = jnp.arange(seq, dtype=F32)[:, None] * inv[None, :]
    cos, sin = jnp.cos(ang), jnp.sin(ang)
    cosf = jnp.concatenate([cos, cos, jnp.ones((seq, LANES - ROPE_DIM), F32)], axis=1)
    sins = jnp.concatenate([-sin, sin, jnp.zeros((seq, LANES - ROPE_DIM), F32)], axis=1)
    return cosf, sins


def _attn_layer(h, norm_g, w_qkv, layer, q_gain, k_gain, lam, subln, w_o, lam_init, tables, *, batch, seq):
    q_gain, k_gain = q_gain.astype(F32), k_gain.astype(F32)
    qkv = _norm_matmul(h, norm_g, w_qkv, layer, mode="qkv", out_dtype=ACT_DTYPE,
                       qkv_extra=(q_gain, k_gain) + tables, seq=seq)
    safe = _scores_are_bounded(q_gain, k_gain)
    o = _diff_attention(qkv.reshape(batch, seq, -1), safe, lam.astype(F32), subln.astype(F32),
                        batch=batch, seq=seq, lam_init=lam_init)
    return _matmul_residual(o.reshape(batch * seq, DA_V_WIDTH), w_o, layer, h)


def _ssm_layer(h, norm_g, w_in, w_dt, layer, conv_w, conv_b, dt_bias, a_log, d_skip, norm_gain, w_out,
               *, batch, seq):
    zx = _norm_matmul(h, norm_g, w_in, layer, mode="plain", out_dtype=ACT_DTYPE, n_out=SSM_ZX_WIDTH)
    dt_raw = _norm_matmul(h, norm_g, w_dt, layer, mode="plain", out_dtype=F32)
    a2_g, eaw_g, b2_t = _ssd_prep(dt_raw, dt_bias, a_log, batch=batch, seq=seq)
    y = _ssd(zx.reshape(batch, seq, SSM_ZX_WIDTH), a2_g, eaw_g, b2_t, conv_w, conv_b, d_skip, norm_gain,
             batch=batch, seq=seq)
    return _matmul_residual(y.reshape(batch * seq, SSM_D_INNER), w_out, layer, h)


def _mlp_layer(h, norm_g, w1, w2, layer):
    a = _norm_matmul(h, norm_g, w1, layer, mode="relu2", out_dtype=ACT_DTYPE)
    return _matmul_residual(a, w2, layer, h)


def kernel(x, mixer_norm, mlp_norm, attn_w_qkv, attn_q_norm, attn_k_norm, attn_lambda, attn_subln, attn_w_o, ssm_w_in, ssm_conv_w, ssm_conv_b, ssm_dt_bias, ssm_a_log, ssm_d, ssm_norm, ssm_w_out, mlp_w1, mlp_w2):
    batch, seq, d_model = x.shape
    depth = mixer_norm.shape[0]
    h = x.reshape(batch * seq, d_model).astype(F32)
    tables = _rope_tables(seq)
    w_qkv = attn_w_qkv.astype(BF16)
    w_o = attn_w_o.astype(BF16)
    w_in = ssm_w_in.astype(BF16)
    w_dt = jnp.pad(ssm_w_in[..., SSM_ZX_WIDTH:], ((0, 0), (0, 0), (0, LANES - SSM_HEADS))).astype(BF16)
    w_out = ssm_w_out.astype(BF16)
    w1 = mlp_w1.astype(BF16)
    w2 = mlp_w2.astype(BF16)
    for i in range(depth):
        j = i // 2
        if i % 2 == 0:
            h = _attn_layer(h, mixer_norm[i].astype(F32), w_qkv, j, attn_q_norm[j], attn_k_norm[j],
                            attn_lambda[j], attn_subln[j], w_o, _lambda_init(i), tables,
                            batch=batch, seq=seq)
        else:
            h = _ssm_layer(h, mixer_norm[i].astype(F32), w_in, w_dt, j, ssm_conv_w[j], ssm_conv_b[j],
                           ssm_dt_bias[j], ssm_a_log[j], ssm_d[j], ssm_norm[j], w_out,
                           batch=batch, seq=seq)
        h = _mlp_layer(h, mlp_norm[i].astype(F32), w1, w2, i)
    return h.reshape(batch, seq, d_model).astype(x.dtype)
```

```python
import functools
import math

import jax
import jax.numpy as jnp
from jax import lax
from jax.experimental import pallas as pl
from jax.experimental.pallas import tpu as pltpu

F32 = jnp.float32
BF16 = jnp.bfloat16

EPS = 1e-6
LANES = 128
SUBLANES = 8
VMEM_LIMIT_BYTES = 56 * 1024 * 1024

DA_HEADS = 8
DA_HEAD_DIM = 128
DA_V_DIM = 2 * DA_HEAD_DIM
DA_QK_WIDTH = DA_HEADS * 2 * DA_HEAD_DIM
DA_V_WIDTH = DA_HEADS * DA_V_DIM
ROPE_THETA = 500000.0
ROPE_DIM = DA_HEAD_DIM // 4
SSM_HEAD_DIM = 64
SSM_GROUPS = 8
SSM_HPG = 8
SSM_D_STATE = 128
SSM_CONV = 4
SSM_CHUNK = 128
SSM_GROUP_WIDTH = SSM_HPG * SSM_HEAD_DIM
SSM_D_INNER = SSM_GROUPS * SSM_GROUP_WIDTH
SSM_HEADS = SSM_GROUPS * SSM_HPG
SSM_BC_WIDTH = SSM_GROUPS * SSM_D_STATE
SSM_ZX_WIDTH = 2 * SSM_D_INNER + 2 * SSM_BC_WIDTH

LOG2_E = math.log2(math.e)
ACT_DTYPE = BF16


def _lambda_init(layer_idx):
    return 0.8 - 0.6 * math.exp(-0.3 * layer_idx)


def _sigmoid(v):
    return 1.0 / (1.0 + jnp.exp(-v))


def _softplus(v):
    return jnp.maximum(v, 0.0) + jnp.log(1.0 + jnp.exp(-jnp.abs(v)))


def _params(sem):
    return pltpu.CompilerParams(dimension_semantics=sem, vmem_limit_bytes=VMEM_LIMIT_BYTES)


def _pick(total, pref):
    t = min(total, pref)
    assert total % t == 0, (total, pref)
    return t


def _rope_partner(v):
    half = ROPE_DIM // 2
    return jnp.concatenate([v[..., half:ROPE_DIM], v[..., :half], v[..., ROPE_DIM:]], axis=-1)


def _rope_group(v, mix, cos_gain, sin_gain):
    res = jnp.dot(jnp.concatenate([v.astype(BF16), (v * v).astype(BF16)], axis=1), mix,
                  preferred_element_type=F32)
    inv = lax.rsqrt(res[:, LANES:] * (1.0 / LANES) + EPS)
    return inv * (v * cos_gain + res[:, :LANES] * sin_gain)


def _norm_mm_kernel(x_ref, g_ref, w_ref, *rest, mode, tn, q_scale):
    if mode == "qkv":
        qg_ref, kg_ref, qgp_ref, kgp_ref, cos_ref, sin_ref, mix_ref, o_ref, xn_ref = rest
    elif mode == "plain_side":
        wside_ref, o_ref, side_ref, xn_ref = rest
    else:
        o_ref, xn_ref = rest
    j = pl.program_id(1)

    @pl.when(j == 0)
    def _():
        x = x_ref[...]
        ms = jnp.mean(x * x, axis=-1, keepdims=True)
        xn_ref[...] = (x * lax.rsqrt(ms + EPS) * g_ref[...]).astype(BF16)

    acc = jnp.dot(xn_ref[...], w_ref[...], preferred_element_type=F32)
    if mode == "plain":
        o_ref[...] = acc.astype(o_ref.dtype)
    elif mode == "plain_side":
        o_ref[...] = acc.astype(o_ref.dtype)

        @pl.when(j == pl.num_programs(1) - 1)
        def _():
            side_ref[...] = jnp.dot(xn_ref[...], wside_ref[...], preferred_element_type=F32)
    elif mode == "relu2":
        r = jnp.maximum(acc, 0.0)
        o_ref[...] = (r * r).astype(o_ref.dtype)
    else:
        n_q = DA_QK_WIDTH // tn

        @pl.when(j < 2 * n_q)
        def _():
            cos_gain = cos_ref[...] * jnp.where(j < n_q, qg_ref[...] * q_scale, kg_ref[...])
            sin_gain = sin_ref[...] * jnp.where(j < n_q, qgp_ref[...] * q_scale, kgp_ref[...])
            mix = mix_ref[...]
            for g in range(tn // LANES):
                sl = slice(g * LANES, (g + 1) * LANES)
                o_ref[:, sl] = _rope_group(acc[:, sl], mix, cos_gain, sin_gain).astype(o_ref.dtype)

        @pl.when(j >= 2 * n_q)
        def _():
            o_ref[...] = acc.astype(o_ref.dtype)


def _norm_matmul(x, gain, w, layer, *, mode, out_dtype, n_out=None, tm_pref=1024, tn_pref=1024,
                 qkv_extra=None, seq=None, w_side=None):
    t, k = x.shape
    n = w.shape[2] if n_out is None else n_out
    tm = _pick(seq if mode == "qkv" else t, tm_pref)
    tn = _pick(n, tn_pref)
    grid = (t // tm, n // tn)
    in_specs = [
        pl.BlockSpec((tm, k), lambda i, j: (i, 0)),
        pl.BlockSpec((1, k), lambda i, j: (0, 0)),
        pl.BlockSpec((None, k, tn), lambda i, j: (layer, 0, j)),
    ]
    args = [x, gain.reshape(1, k), w]
    if mode == "qkv":
        qg, kg, cosf, sins = qkv_extra
        assert seq % tm == 0 and DA_QK_WIDTH % tn == 0
        n_s = seq // tm
        lane = jnp.arange(LANES)
        perm = (lane[:, None] == _rope_partner(lane)[None, :]) & (lane[None, :] < ROPE_DIM)
        zeros = jnp.zeros((LANES, LANES), F32)
        mix = jnp.block([[perm.astype(F32), zeros], [zeros, jnp.ones((LANES, LANES), F32)]]).astype(BF16)
        vec = pl.BlockSpec((1, LANES), lambda i, j: (0, 0))
        tab = pl.BlockSpec((tm, LANES), lambda i, j: (i % n_s, 0))
        in_specs += [vec, vec, vec, vec, tab, tab, pl.BlockSpec((2 * LANES, 2 * LANES), lambda i, j: (0, 0))]
        args += [qg.reshape(1, LANES), kg.reshape(1, LANES), _rope_partner(qg).reshape(1, LANES),
                 _rope_partner(kg).reshape(1, LANES), cosf, sins, mix]
    out_specs = pl.BlockSpec((tm, tn), lambda i, j: (i, j))
    out_shape = jax.ShapeDtypeStruct((t, n), out_dtype)
    if mode == "plain_side":
        n_side = w_side.shape[2]
        in_specs.append(pl.BlockSpec((None, k, n_side), lambda i, j: (layer, 0, 0)))
        args.append(w_side)
        out_specs = [out_specs, pl.BlockSpec((tm, n_side), lambda i, j: (i, 0))]
        out_shape = [out_shape, jax.ShapeDtypeStruct((t, n_side), F32)]
    return pl.pallas_call(
        functools.partial(_norm_mm_kernel, mode=mode, tn=tn, q_scale=DA_HEAD_DIM ** -0.5 * LOG2_E),
        grid=grid,
        in_specs=in_specs,
        out_specs=out_specs,
        out_shape=out_shape,
        scratch_shapes=[pltpu.VMEM((tm, k), BF16)],
        compiler_params=_params(("parallel", "arbitrary")),
        name="norm_mm_" + mode,
    )(*args)


def _mm_res_kernel(x_ref, w_ref, r_ref, o_ref, *scratch, nk):
    p = jnp.dot(x_ref[...], w_ref[...], preferred_element_type=F32)
    if nk == 1:
        o_ref[...] = r_ref[...] + p
        return
    acc_ref, = scratch
    k = pl.program_id(2)

    @pl.when(k == 0)
    def _():
        acc_ref[...] = p

    if nk > 2:
        @pl.when((k > 0) & (k < nk - 1))
        def _():
            acc_ref[...] += p

    @pl.when(k == nk - 1)
    def _():
        o_ref[...] = r_ref[...] + (acc_ref[...] + p)


def _matmul_residual(x, w, layer, res, *, tm_pref=1024, tn_pref=1024, tk_pref=2048):
    t, k = x.shape
    n = w.shape[2]
    tm, tn, tk = _pick(t, tm_pref), _pick(n, tn_pref), _pick(k, tk_pref)
    nk = k // tk
    return pl.pallas_call(
        functools.partial(_mm_res_kernel, nk=nk),
        grid=(t // tm, n // tn, nk),
        in_specs=[
            pl.BlockSpec((tm, tk), lambda i, j, kk: (i, kk)),
            pl.BlockSpec((None, tk, tn), lambda i, j, kk: (layer, kk, j)),
            pl.BlockSpec((tm, tn), lambda i, j, kk: (i, j)),
        ],
        out_specs=pl.BlockSpec((tm, tn), lambda i, j, kk: (i, j)),
        out_shape=jax.ShapeDtypeStruct((t, n), F32),
        scratch_shapes=[pltpu.VMEM((tm, tn), F32)] if nk > 1 else [],
        compiler_params=_params(("parallel", "parallel", "arbitrary")),
        name="mm_res",
    )(x, w, res)


NEG_BIG = -1e30
MAX_UNSHIFTED_SCORE = 64.0


def _attn_kernel(safe_ref, lam_ref, sg_ref, q_ref, k_ref, v_ref, o_ref, m_ref, l_ref, acc_ref, lp_ref,
                 *, tq, lam_init):
    qi = pl.program_id(2)
    d = DA_HEAD_DIM
    wide = 2 * tq
    safe = safe_ref[0] == 1

    def scores(c, kb, width, diag_col):
        qc = q_ref[0, :, c * d:(c + 1) * d]
        s = lax.dot_general(qc, kb[:, c * d:(c + 1) * d], (((1,), (1,)), ((), ())),
                            preferred_element_type=F32)
        if diag_col is None:
            return s, None
        row = lax.broadcasted_iota(jnp.int32, (tq, width), 0)
        col = lax.broadcasted_iota(jnp.int32, (tq, width), 1)
        return s, col <= row + diag_col

    def step_unshifted(start, width, diag_col, first=False):
        kb = k_ref[0, pl.ds(start, width), :]
        vb = v_ref[0, pl.ds(start, width), :]
        for c in range(2):
            s, keep = scores(c, kb, width, diag_col)
            p = jnp.exp2(s)
            if keep is not None:
                p = jnp.where(keep, p, 0.0)
            part = p[:, 0:LANES]
            for g in range(1, width // LANES):
                part = part + p[:, g * LANES:(g + 1) * LANES]
            pv = jnp.dot(p.astype(BF16), vb, preferred_element_type=F32)
            if first:
                lp_ref[c] = part
                acc_ref[c] = pv
            else:
                lp_ref[c] += part
                acc_ref[c] += pv

    def step_online(start, width, diag_col):
        kb = k_ref[0, pl.ds(start, width), :]
        vb = v_ref[0, pl.ds(start, width), :]
        for c in range(2):
            s, keep = scores(c, kb, width, diag_col)
            if keep is not None:
                s = jnp.where(keep, s, NEG_BIG)
            m_old = m_ref[c]
            m_new = jnp.maximum(m_old, jnp.max(s, axis=-1, keepdims=True))
            alpha = jnp.exp2(m_old - m_new)
            p = jnp.exp2(s - m_new)
            l_ref[c] = alpha * l_ref[c] + jnp.sum(p, axis=-1, keepdims=True)
            acc_ref[c] = alpha * acc_ref[c] + jnp.dot(p.astype(BF16), vb, preferred_element_type=F32)
            m_ref[c] = m_new

    def diagonal(step, **kw):
        @pl.when(qi % 2 == 1)
        def _():
            step(pl.multiple_of((qi - 1) * tq, tq), wide, tq, **kw)

        @pl.when(qi % 2 == 0)
        def _():
            step(pl.multiple_of(qi * tq, tq), tq, 0, **kw)

    def off_diagonal(step):
        def body(j, carry):
            step(pl.multiple_of(j * wide, wide), wide, None)
            return carry

        lax.fori_loop(0, qi // 2, body, 0)

    @pl.when(safe)
    def _():
        diagonal(step_unshifted, first=True)
        off_diagonal(step_unshifted)
        l_ref[...] = jnp.sum(lp_ref[...], axis=-1, keepdims=True)

    @pl.when(jnp.logical_not(safe))
    def _():
        m_ref[...] = jnp.full(m_ref.shape, NEG_BIG, F32)
        l_ref[...] = jnp.zeros(l_ref.shape, F32)
        acc_ref[...] = jnp.zeros(acc_ref.shape, F32)
        off_diagonal(step_online)
        diagonal(step_online)

    lam = lam_ref[...]
    lam_full = (jnp.exp(jnp.sum(lam[0:1] * lam[1:2], axis=-1, keepdims=True))
                - jnp.exp(jnp.sum(lam[2:3] * lam[3:4], axis=-1, keepdims=True)) + lam_init)
    o = acc_ref[0] / l_ref[0] - lam_full * (acc_ref[1] / l_ref[1])
    ms = jnp.mean(o * o, axis=-1, keepdims=True)
    o_ref[0] = (o * lax.rsqrt(ms + EPS) * (sg_ref[...] * (1.0 - lam_init))).astype(o_ref.dtype)


def _scores_are_bounded(q_gain, k_gain):
    bound = (jnp.max(jnp.abs(q_gain)) * jnp.max(jnp.abs(k_gain))
             * (DA_HEAD_DIM * DA_HEAD_DIM ** -0.5 * LOG2_E * 1.02))
    return (bound <= MAX_UNSHIFTED_SCORE).astype(jnp.int32).reshape(1)


def _diff_attention(qkv, safe, lam, subln, *, batch, seq, lam_init, tq_pref=512):
    tq = _pick(seq, tq_pref)
    hw = 2 * DA_HEAD_DIM
    kern = functools.partial(_attn_kernel, tq=tq, lam_init=lam_init)
    return pl.pallas_call(
        kern,
        grid=(batch, DA_HEADS, seq // tq),
        in_specs=[
            pl.BlockSpec(memory_space=pltpu.SMEM),
            pl.BlockSpec((4, DA_HEAD_DIM), lambda b, h, i: (0, 0)),
            pl.BlockSpec((1, DA_V_DIM), lambda b, h, i: (0, 0)),
            pl.BlockSpec((1, tq, hw), lambda b, h, i: (b, i, h)),
            pl.BlockSpec((1, seq, hw), lambda b, h, i: (b, 0, DA_HEADS + h)),
            pl.BlockSpec((1, seq, DA_V_DIM), lambda b, h, i: (b, 0, 2 * DA_HEADS + h)),
        ],
        out_specs=pl.BlockSpec((1, tq, DA_V_DIM), lambda b, h, i: (b, i, h)),
        out_shape=jax.ShapeDtypeStruct((batch, seq, DA_V_WIDTH), ACT_DTYPE),
        scratch_shapes=[
            pltpu.VMEM((2, tq, 1), F32),
            pltpu.VMEM((2, tq, 1), F32),
            pltpu.VMEM((2, tq, DA_V_DIM), F32),
            pltpu.VMEM((2, tq, LANES), F32),
        ],
        compiler_params=_params(("parallel", "parallel", "arbitrary")),
        name="diff_attn",
    )(safe, lam, subln.reshape(1, DA_V_DIM), qkv, qkv, qkv)


def _split3(v):
    hi = v.astype(BF16)
    r1 = v - hi.astype(F32)
    mid = r1.astype(BF16)
    lo = (r1 - mid.astype(F32)).astype(BF16)
    return hi, mid, lo


def _ssd_prep_kernel(dt_ref, bias_ref, alog_ref, sel_ref, a2g_ref, eaw_ref, b2t_ref, *, ts):
    L = SSM_CHUNK
    dt = _softplus(dt_ref[0] + bias_ref[...])
    dta2 = dt * (-LOG2_E * jnp.exp(alog_ref[...]))
    ld2 = jnp.log(dt) * LOG2_E
    row = lax.broadcasted_iota(jnp.int32, (L, L), 0)
    col = lax.broadcasted_iota(jnp.int32, (L, L), 1)
    tril = jnp.where(row >= col, 1.0, 0.0).astype(BF16)
    for ci in range(ts // L):
        sl = slice(ci * L, (ci + 1) * L)
        hi, mid, lo = _split3(dta2[sl])
        a2 = (jnp.dot(tril, lo, preferred_element_type=F32) + jnp.dot(tril, mid, preferred_element_type=F32)
              + jnp.dot(tril, hi, preferred_element_type=F32))
        b2 = a2 - ld2[sl]
        b2t_ref[0, :, sl] = b2.T
        ea_w = jnp.concatenate([jnp.exp2(a2).astype(BF16), jnp.exp2(a2[L - 1:L, :] - b2).astype(BF16)], axis=1)
        for g in range(SSM_GROUPS):
            a2g_ref[0, g, sl, :] = a2[:, g * SSM_HPG:(g + 1) * SSM_HPG]
            eaw_ref[0, g, sl, :] = jnp.dot(ea_w, sel_ref[g], preferred_element_type=F32).astype(BF16)


def _ssd_prep(dt_raw, dt_bias, a_log, *, batch, seq, ts_pref=512):
    ts = _pick(seq, ts_pref)
    pad = LANES - SSM_HEADS
    bias = jnp.pad(dt_bias.astype(F32), (0, pad)).reshape(1, LANES)
    alog = jnp.pad(a_log.astype(F32), (0, pad)).reshape(1, LANES)
    r = jnp.arange(2 * LANES)[None, :, None]
    j = jnp.arange(LANES)[None, None, :]
    g = jnp.arange(SSM_GROUPS)[:, None, None]
    sel = (((r == g * SSM_HPG + j) & (j < SSM_HPG))
           | ((r == LANES + g * SSM_HPG + j - SSM_HPG) & (j >= SSM_HPG) & (j < 2 * SSM_HPG))).astype(BF16)
    col = pl.BlockSpec((1, ts, LANES), lambda b, c: (b, c, 0))
    rowl = pl.BlockSpec((1, LANES, ts), lambda b, c: (b, 0, c))
    vec = pl.BlockSpec((1, LANES), lambda b, c: (0, 0))
    return pl.pallas_call(
        functools.partial(_ssd_prep_kernel, ts=ts),
        grid=(batch, seq // ts),
        in_specs=[col, vec, vec, pl.BlockSpec((SSM_GROUPS, 2 * LANES, LANES), lambda b, c: (0, 0, 0))],
        out_specs=[pl.BlockSpec((1, SSM_GROUPS, ts, SSM_HPG), lambda b, c: (b, 0, c, 0)),
                   pl.BlockSpec((1, SSM_GROUPS, ts, LANES), lambda b, c: (b, 0, c, 0)),
                   rowl],
        out_shape=[jax.ShapeDtypeStruct((batch, SSM_GROUPS, seq, SSM_HPG), F32),
                   jax.ShapeDtypeStruct((batch, SSM_GROUPS, seq, LANES), BF16),
                   jax.ShapeDtypeStruct((batch, LANES, seq), F32)],
        compiler_params=_params(("parallel", "parallel")),
        name="ssd_prep",
    )(dt_raw.reshape(batch, seq, LANES), bias, alog, sel)


def _ssd_kernel(x_ref, z_ref, b_ref, c_ref, cwx_ref, cwb_ref, cwc_ref, cbx_ref, cbb_ref, cbc_ref,
                a2g_ref, eaw_ref, b2t_ref, shift_ref, expand_ref, d_ref, ng_ref, o_ref, tail_ref, st_ref, *, ts):
    L = SSM_CHUNK
    gw, n = SSM_GROUP_WIDTH, SSM_D_STATE
    cidx = pl.program_id(2)

    slot = cidx % 2

    @pl.when(cidx == 0)
    def _():
        tail_ref[0] = jnp.zeros(tail_ref.shape[1:], tail_ref.dtype)
        st_ref[...] = jnp.zeros(st_ref.shape, F32)

    def rows(lo, hi):
        return jnp.concatenate([x_ref[0, lo:hi, :], b_ref[0, lo:hi, :], c_ref[0, lo:hi, :]], axis=1)

    row = lax.broadcasted_iota(jnp.int32, (L, L), 0)
    col = lax.broadcasted_iota(jnp.int32, (L, L), 1)
    causal = row >= col
    low_half = lax.broadcasted_iota(jnp.int32, (1, LANES), 1) < SSM_HEAD_DIM
    lane = lax.broadcasted_iota(jnp.int32, (L, LANES), 1)
    keep_lo = jnp.where(lane < SSM_HEAD_DIM, 1.0, 0.0).astype(BF16)
    keep_hi = jnp.where(lane < SSM_HEAD_DIM, 0.0, 1.0).astype(BF16)

    n_shift = SSM_CONV - 1
    conv_w = jnp.concatenate([cwx_ref[...], cwb_ref[...], cwc_ref[...]], axis=1)
    conv_b = jnp.concatenate([cbx_ref[...], cbb_ref[...], cbc_ref[...]], axis=1)

    st = st_ref[...]
    for ci in range(ts // L):
        r0 = ci * L
        win = jnp.concatenate([tail_ref[slot], rows(0, L)], axis=0) if ci == 0 else rows(r0 - L, r0 + L)
        lagged = jnp.dot(shift_ref[...], win, preferred_element_type=F32)
        acc = conv_b + conv_w[n_shift:n_shift + 1, :] * win[L:2 * L, :].astype(F32)
        for k in range(n_shift):
            acc = acc + conv_w[k:k + 1, :] * lagged[k * L:(k + 1) * L, :]
        act = acc * _sigmoid(acc)
        xc, bc, cc = act[:, 0:gw], act[:, gw:gw + n], act[:, gw + n:gw + 2 * n]
        cb = lax.dot_general(cc.astype(BF16), bc.astype(BF16), (((1,), (1,)), ((), ())),
                             preferred_element_type=F32)
        cbb = cb.astype(BF16)
        btb = bc.T.astype(BF16)
        a2c = a2g_ref[0, 0, r0:r0 + L, :]
        b2r = b2t_ref[0, :, r0:r0 + L]
        dlast = jnp.exp2(a2c[L - 1:L, :])
        spread = jnp.dot(eaw_ref[0, 0, r0:r0 + L, :], expand_ref[...], preferred_element_type=F32)
        ea_x, w_x = spread[:, 0:gw], spread[:, gw:2 * gw]
        xcb = xc.astype(BF16)
        y_off = jnp.dot(cc.astype(BF16), st.astype(BF16), preferred_element_type=F32)
        xw = (xc * w_x).astype(BF16)
        s_new = jnp.dot(btb, xw, preferred_element_type=F32)
        ys, ds = [], []
        for j in range(SSM_HPG // 2):
            xpair = xcb[:, j * LANES:(j + 1) * LANES]
            rhs = jnp.concatenate([xpair * keep_lo, xpair * keep_hi], axis=0)
            ms2 = []
            for h in (2 * j, 2 * j + 1):
                seg = a2c[:, h:h + 1] - b2r[h:h + 1, :]
                ms2.append(cbb * jnp.exp2(jnp.where(causal, seg, -jnp.inf)).astype(BF16))
            ys.append(jnp.dot(jnp.concatenate(ms2, axis=1), rhs, preferred_element_type=F32))
            ds.append(jnp.where(low_half, dlast[:, 2 * j:2 * j + 1], dlast[:, 2 * j + 1:2 * j + 2]))
        st = st * jnp.concatenate(ds, axis=1) + s_new
        y = jnp.concatenate(ys, axis=1) + y_off * ea_x + d_ref[...] * xc
        zz = z_ref[0, r0:r0 + L, :].astype(F32)
        y = y * (zz * _sigmoid(zz))
        ms = jnp.mean(y * y, axis=-1, keepdims=True)
        o_ref[0, r0:r0 + L, :] = (y * lax.rsqrt(ms + EPS) * ng_ref[...]).astype(o_ref.dtype)

    st_ref[...] = st
    tail_ref[1 - slot] = rows(ts - L, ts)


def _ssd(zx, a2_g, eaw_g, b2_t, conv_w, conv_b, d_skip, norm_gain, *, batch, seq, ts_pref=1024):
    assert zx.dtype == BF16, "the shift-matrix conv is exact only for bf16 inputs"
    ts = _pick(seq, ts_pref)
    gw, n, G = SSM_GROUP_WIDTH, SSM_D_STATE, SSM_GROUPS
    L, n_shift = SSM_CHUNK, SSM_CONV - 1
    srow = jnp.arange(n_shift * L)[:, None]
    shift = (jnp.arange(2 * L)[None, :] == (srow % L) + (L - n_shift) + srow // L).astype(BF16)
    er = jnp.arange(LANES)[:, None]
    ec = jnp.arange(2 * gw)[None, :]
    expand = (er == (ec % gw) // SSM_HEAD_DIM + SSM_HPG * (ec // gw)).astype(BF16)
    x_blk0 = SSM_D_INNER // gw
    b_blk0 = 2 * SSM_D_INNER // n
    c_blk0 = b_blk0 + SSM_BC_WIDTH // n
    cw = conv_w.astype(F32)
    cbias = conv_b.astype(F32).reshape(1, -1)
    d_exp = jnp.repeat(d_skip.astype(F32), SSM_HEAD_DIM).reshape(1, SSM_D_INNER)
    ng = norm_gain.astype(F32).reshape(1, SSM_D_INNER)
    cwb0 = SSM_D_INNER // n
    cwc0 = cwb0 + SSM_BC_WIDTH // n
    in_specs = [
        pl.BlockSpec((1, ts, gw), lambda b, g, c: (b, c, x_blk0 + g)),
        pl.BlockSpec((1, ts, gw), lambda b, g, c: (b, c, g)),
        pl.BlockSpec((1, ts, n), lambda b, g, c: (b, c, b_blk0 + g)),
        pl.BlockSpec((1, ts, n), lambda b, g, c: (b, c, c_blk0 + g)),
        pl.BlockSpec((SSM_CONV, gw), lambda b, g, c: (0, g)),
        pl.BlockSpec((SSM_CONV, n), lambda b, g, c: (0, cwb0 + g)),
        pl.BlockSpec((SSM_CONV, n), lambda b, g, c: (0, cwc0 + g)),
        pl.BlockSpec((1, gw), lambda b, g, c: (0, g)),
        pl.BlockSpec((1, n), lambda b, g, c: (0, cwb0 + g)),
        pl.BlockSpec((1, n), lambda b, g, c: (0, cwc0 + g)),
        pl.BlockSpec((1, 1, ts, SSM_HPG), lambda b, g, c: (b, g, c, 0)),
        pl.BlockSpec((1, 1, ts, LANES), lambda b, g, c: (b, g, c, 0)),
        pl.BlockSpec((1, SSM_HPG, ts), lambda b, g, c: (b, g, c)),
        pl.BlockSpec((n_shift * L, 2 * L), lambda b, g, c: (0, 0)),
        pl.BlockSpec((LANES, 2 * gw), lambda b, g, c: (0, 0)),
        pl.BlockSpec((1, gw), lambda b, g, c: (0, g)),
        pl.BlockSpec((1, gw), lambda b, g, c: (0, g)),
    ]
    return pl.pallas_call(
        functools.partial(_ssd_kernel, ts=ts),
        grid=(batch, G, seq // ts),
        in_specs=in_specs,
        out_specs=pl.BlockSpec((1, ts, gw), lambda b, g, c: (b, c, g)),
        out_shape=jax.ShapeDtypeStruct((batch, seq, SSM_D_INNER), ACT_DTYPE),
        scratch_shapes=[
            pltpu.VMEM((2, SSM_CHUNK, gw + 2 * n), BF16),
            pltpu.VMEM((n, gw), F32),
        ],
        compiler_params=_params(("parallel", "parallel", "arbitrary")),
        name="ssd_scan",
    )(zx, zx, zx, zx, cw, cw, cw, cbias, cbias, cbias, a2_g, eaw_g, b2_t, shift, expand, d_exp, ng)


def _rope_tables(seq):
    inv = ROPE_THETA ** (-jnp.arange(0, ROPE_DIM, 2, dtype=F32) / ROPE_DIM)
    ang = jnp.arange(seq, dtype=F32)[:, None] * inv[None, :]
    cos, sin = jnp.cos(ang), jnp.sin(ang)
    cosf = jnp.concatenate([cos, cos, jnp.ones((seq, LANES - ROPE_DIM), F32)], axis=1)
    sins = jnp.concatenate([-sin, sin, jnp.zeros((seq, LANES - ROPE_DIM), F32)], axis=1)
    return cosf, sins


def _attn_layer(h, norm_g, w_qkv, layer, q_gain, k_gain, lam, subln, w_o, lam_init, tables, *, batch, seq):
    q_gain, k_gain = q_gain.astype(F32), k_gain.astype(F32)
    qkv = _norm_matmul(h, norm_g, w_qkv, layer, mode="qkv", out_dtype=ACT_DTYPE,
                       qkv_extra=(q_gain, k_gain) + tables, seq=seq)
    safe = _scores_are_bounded(q_gain, k_gain)
    o = _diff_attention(qkv.reshape(batch, seq, -1), safe, lam.astype(F32), subln.astype(F32),
                        batch=batch, seq=seq, lam_init=lam_init)
    return _matmul_residual(o.reshape(batch * seq, DA_V_WIDTH), w_o, layer, h)


def _ssm_layer(h, norm_g, w_in, w_dt, layer, conv_w, conv_b, dt_bias, a_log, d_skip, norm_gain, w_out,
               *, batch, seq):
    zx, dt_raw = _norm_matmul(h, norm_g, w_in, layer, mode="plain_side", out_dtype=ACT_DTYPE, n_out=SSM_ZX_WIDTH,
                              w_side=w_dt)
    a2_g, eaw_g, b2_t = _ssd_prep(dt_raw, dt_bias, a_log, batch=batch, seq=seq)
    y = _ssd(zx.reshape(batch, seq, SSM_ZX_WIDTH), a2_g, eaw_g, b2_t, conv_w, conv_b, d_skip, norm_gain,
             batch=batch, seq=seq)
    return _matmul_residual(y.reshape(batch * seq, SSM_D_INNER), w_out, layer, h)


def _mlp_layer(h, norm_g, w1, w2, layer):
    a = _norm_matmul(h, norm_g, w1, layer, mode="relu2", out_dtype=ACT_DTYPE)
    return _matmul_residual(a, w2, layer, h)


def kernel(x, mixer_norm, mlp_norm, attn_w_qkv, attn_q_norm, attn_k_norm, attn_lambda, attn_subln, attn_w_o, ssm_w_in, ssm_conv_w, ssm_conv_b, ssm_dt_bias, ssm_a_log, ssm_d, ssm_norm, ssm_w_out, mlp_w1, mlp_w2):
    batch, seq, d_model = x.shape
    depth = mixer_norm.shape[0]
    h = x.reshape(batch * seq, d_model).astype(F32)
    tables = _rope_tables(seq)
    w_qkv = attn_w_qkv.astype(BF16)
    w_o = attn_w_o.astype(BF16)
    w_in = ssm_w_in.astype(BF16)
    w_dt = jnp.pad(ssm_w_in[..., SSM_ZX_WIDTH:], ((0, 0), (0, 0), (0, LANES - SSM_HEADS))).astype(BF16)
    w_out = ssm_w_out.astype(BF16)
    w1 = mlp_w1.astype(BF16)
    w2 = mlp_w2.astype(BF16)
    for i in range(depth):
        j = i // 2
        if i % 2 == 0:
            h = _attn_layer(h, mixer_norm[i].astype(F32), w_qkv, j, attn_q_norm[j], attn_k_norm[j],
                            attn_lambda[j], attn_subln[j], w_o, _lambda_init(i), tables,
                            batch=batch, seq=seq)
        else:
            h = _ssm_layer(h, mixer_norm[i].astype(F32), w_in, w_dt, j, ssm_conv_w[j], ssm_conv_b[j],
                           ssm_dt_bias[j], ssm_a_log[j], ssm_d[j], ssm_norm[j], w_out,
                           batch=batch, seq=seq)
        h = _mlp_layer(h, mlp_norm[i].astype(F32), w1, w2, i)
    return h.reshape(batch, seq, d_model).astype(x.dtype)
```

```python
import functools
import math

import jax
import jax.numpy as jnp
from jax import lax
from jax.experimental import pallas as pl
from jax.experimental.pallas import tpu as pltpu

F32 = jnp.float32
BF16 = jnp.bfloat16

EPS = 1e-6
LANES = 128
SUBLANES = 8
VMEM_LIMIT_BYTES = 56 * 1024 * 1024

DA_HEADS = 8
DA_HEAD_DIM = 128
DA_V_DIM = 2 * DA_HEAD_DIM
DA_QK_WIDTH = DA_HEADS * 2 * DA_HEAD_DIM
DA_V_WIDTH = DA_HEADS * DA_V_DIM
ROPE_THETA = 500000.0
ROPE_DIM = DA_HEAD_DIM // 4
SSM_HEAD_DIM = 64
SSM_GROUPS = 8
SSM_HPG = 8
SSM_D_STATE = 128
SSM_CONV = 4
SSM_CHUNK = 128
SSM_GROUP_WIDTH = SSM_HPG * SSM_HEAD_DIM
SSM_D_INNER = SSM_GROUPS * SSM_GROUP_WIDTH
SSM_HEADS = SSM_GROUPS * SSM_HPG
SSM_BC_WIDTH = SSM_GROUPS * SSM_D_STATE
SSM_ZX_WIDTH = 2 * SSM_D_INNER + 2 * SSM_BC_WIDTH

LOG2_E = math.log2(math.e)
ACT_DTYPE = BF16


def _lambda_init(layer_idx):
    return 0.8 - 0.6 * math.exp(-0.3 * layer_idx)


def _sigmoid(v):
    return 1.0 / (1.0 + jnp.exp(-v))


def _softplus(v):
    return jnp.maximum(v, 0.0) + jnp.log(1.0 + jnp.exp(-jnp.abs(v)))


def _params(sem):
    return pltpu.CompilerParams(dimension_semantics=sem, vmem_limit_bytes=VMEM_LIMIT_BYTES)


def _pick(total, pref):
    t = min(total, pref)
    assert total % t == 0, (total, pref)
    return t


def _rope_partner(v):
    half = ROPE_DIM // 2
    return jnp.concatenate([v[..., half:ROPE_DIM], v[..., :half], v[..., ROPE_DIM:]], axis=-1)


def _rope_group(v, mix, cos_gain, sin_gain):
    res = jnp.dot(jnp.concatenate([v.astype(BF16), (v * v).astype(BF16)], axis=1), mix,
                  preferred_element_type=F32)
    inv = lax.rsqrt(res[:, LANES:] * (1.0 / LANES) + EPS)
    return inv * (v * cos_gain + res[:, :LANES] * sin_gain)


def _norm_mm_kernel(x_ref, g_ref, w_ref, *rest, mode, tn, q_scale):
    if mode == "qkv":
        qg_ref, kg_ref, qgp_ref, kgp_ref, cos_ref, sin_ref, mix_ref, o_ref, xn_ref = rest
    elif mode == "plain_side":
        wside_ref, o_ref, side_ref, xn_ref = rest
    else:
        o_ref, xn_ref = rest
    j = pl.program_id(1)

    @pl.when(j == 0)
    def _():
        x = x_ref[...]
        ms = jnp.mean(x * x, axis=-1, keepdims=True)
        xn_ref[...] = (x * lax.rsqrt(ms + EPS) * g_ref[...]).astype(BF16)

    acc = jnp.dot(xn_ref[...], w_ref[...], preferred_element_type=F32)
    if mode == "plain":
        o_ref[...] = acc.astype(o_ref.dtype)
    elif mode == "plain_side":
        o_ref[...] = acc.astype(o_ref.dtype)

        @pl.when(j == pl.num_programs(1) - 1)
        def _():
            side_ref[...] = jnp.dot(xn_ref[...], wside_ref[...], preferred_element_type=F32)
    elif mode == "relu2":
        r = jnp.maximum(acc, 0.0)
        o_ref[...] = (r * r).astype(o_ref.dtype)
    else:
        n_q = DA_QK_WIDTH // tn

        @pl.when(j < 2 * n_q)
        def _():
            cos_gain = cos_ref[...] * jnp.where(j < n_q, qg_ref[...] * q_scale, kg_ref[...])
            sin_gain = sin_ref[...] * jnp.where(j < n_q, qgp_ref[...] * q_scale, kgp_ref[...])
            mix = mix_ref[...]
            for g in range(tn // LANES):
                sl = slice(g * LANES, (g + 1) * LANES)
                o_ref[:, sl] = _rope_group(acc[:, sl], mix, cos_gain, sin_gain).astype(o_ref.dtype)

        @pl.when(j >= 2 * n_q)
        def _():
            o_ref[...] = acc.astype(o_ref.dtype)


def _norm_matmul(x, gain, w, layer, *, mode, out_dtype, n_out=None, tm_pref=1024, tn_pref=1024,
                 qkv_extra=None, seq=None, w_side=None):
    t, k = x.shape
    n = w.shape[2] if n_out is None else n_out
    tm = _pick(seq if mode == "qkv" else t, tm_pref)
    tn = _pick(n, tn_pref)
    grid = (t // tm, n // tn)
    in_specs = [
        pl.BlockSpec((tm, k), lambda i, j: (i, 0)),
        pl.BlockSpec((1, k), lambda i, j: (0, 0)),
        pl.BlockSpec((None, k, tn), lambda i, j: (layer, 0, j)),
    ]
    args = [x, gain.reshape(1, k), w]
    if mode == "qkv":
        qg, kg, cosf, sins = qkv_extra
        assert seq % tm == 0 and DA_QK_WIDTH % tn == 0
        n_s = seq // tm
        lane = jnp.arange(LANES)
        perm = (lane[:, None] == _rope_partner(lane)[None, :]) & (lane[None, :] < ROPE_DIM)
        zeros = jnp.zeros((LANES, LANES), F32)
        mix = jnp.block([[perm.astype(F32), zeros], [zeros, jnp.ones((LANES, LANES), F32)]]).astype(BF16)
        vec = pl.BlockSpec((1, LANES), lambda i, j: (0, 0))
        tab = pl.BlockSpec((tm, LANES), lambda i, j: (i % n_s, 0))
        in_specs += [vec, vec, vec, vec, tab, tab, pl.BlockSpec((2 * LANES, 2 * LANES), lambda i, j: (0, 0))]
        args += [qg.reshape(1, LANES), kg.reshape(1, LANES), _rope_partner(qg).reshape(1, LANES),
                 _rope_partner(kg).reshape(1, LANES), cosf, sins, mix]
    out_specs = pl.BlockSpec((tm, tn), lambda i, j: (i, j))
    out_shape = jax.ShapeDtypeStruct((t, n), out_dtype)
    if mode == "plain_side":
        n_side = w_side.shape[2]
        in_specs.append(pl.BlockSpec((None, k, n_side), lambda i, j: (layer, 0, 0)))
        args.append(w_side)
        out_specs = [out_specs, pl.BlockSpec((tm, n_side), lambda i, j: (i, 0))]
        out_shape = [out_shape, jax.ShapeDtypeStruct((t, n_side), F32)]
    return pl.pallas_call(
        functools.partial(_norm_mm_kernel, mode=mode, tn=tn, q_scale=DA_HEAD_DIM ** -0.5 * LOG2_E),
        grid=grid,
        in_specs=in_specs,
        out_specs=out_specs,
        out_shape=out_shape,
        scratch_shapes=[pltpu.VMEM((tm, k), BF16)],
        compiler_params=_params(("parallel", "arbitrary")),
        name="norm_mm_" + mode,
    )(*args)


def _mm_res_kernel(x_ref, w_ref, r_ref, o_ref, *, nk):
    p = jnp.dot(x_ref[...], w_ref[...], preferred_element_type=F32)
    if nk == 1:
        o_ref[...] = r_ref[...] + p
        return
    k = pl.program_id(2)

    @pl.when(k == 0)
    def _():
        o_ref[...] = r_ref[...] + p

    @pl.when(k > 0)
    def _():
        o_ref[...] += p


def _matmul_residual(x, w, layer, res, *, tm_pref=1024, tn_pref=1024, tk_pref=2048):
    t, k = x.shape
    n = w.shape[2]
    tm, tn, tk = _pick(t, tm_pref), _pick(n, tn_pref), _pick(k, tk_pref)
    nk = k // tk
    return pl.pallas_call(
        functools.partial(_mm_res_kernel, nk=nk),
        grid=(t // tm, n // tn, nk),
        in_specs=[
            pl.BlockSpec((tm, tk), lambda i, j, kk: (i, kk)),
            pl.BlockSpec((None, tk, tn), lambda i, j, kk: (layer, kk, j)),
            pl.BlockSpec((tm, tn), lambda i, j, kk: (i, j)),
        ],
        out_specs=pl.BlockSpec((tm, tn), lambda i, j, kk: (i, j)),
        out_shape=jax.ShapeDtypeStruct((t, n), F32),
        compiler_params=_params(("parallel", "parallel", "arbitrary")),
        name="mm_res",
    )(x, w, res)


NEG_BIG = -1e30
MAX_UNSHIFTED_SCORE = 64.0


def _attn_kernel(safe_ref, lam_ref, sg_ref, q_ref, k_ref, v_ref, o_ref, m_ref, l_ref, acc_ref, lp_ref,
                 *, tq, lam_init):
    qi = pl.program_id(2)
    d = DA_HEAD_DIM
    wide = 2 * tq
    safe = safe_ref[0] == 1

    def scores(c, kb, width, diag_col):
        qc = q_ref[0, :, c * d:(c + 1) * d]
        s = lax.dot_general(qc, kb[:, c * d:(c + 1) * d], (((1,), (1,)), ((), ())),
                            preferred_element_type=F32)
        if diag_col is None:
            return s, None
        row = lax.broadcasted_iota(jnp.int32, (tq, width), 0)
        col = lax.broadcasted_iota(jnp.int32, (tq, width), 1)
        return s, col <= row + diag_col

    def step_unshifted(start, width, diag_col, first=False):
        kb = k_ref[0, pl.ds(start, width), :]
        vb = v_ref[0, pl.ds(start, width), :]
        for c in range(2):
            s, keep = scores(c, kb, width, diag_col)
            p = jnp.exp2(s)
            if keep is not None:
                p = jnp.where(keep, p, 0.0)
            part = p[:, 0:LANES]
            for g in range(1, width // LANES):
                part = part + p[:, g * LANES:(g + 1) * LANES]
            pv = jnp.dot(p.astype(BF16), vb, preferred_element_type=F32)
            if first:
                lp_ref[c] = part
                acc_ref[c] = pv
            else:
                lp_ref[c] += part
                acc_ref[c] += pv

    def step_online(start, width, diag_col):
        kb = k_ref[0, pl.ds(start, width), :]
        vb = v_ref[0, pl.ds(start, width), :]
        for c in range(2):
            s, keep = scores(c, kb, width, diag_col)
            if keep is not None:
                s = jnp.where(keep, s, NEG_BIG)
            m_old = m_ref[c]
            m_new = jnp.maximum(m_old, jnp.max(s, axis=-1, keepdims=True))
            alpha = jnp.exp2(m_old - m_new)
            p = jnp.exp2(s - m_new)
            l_ref[c] = alpha * l_ref[c] + jnp.sum(p, axis=-1, keepdims=True)
            acc_ref[c] = alpha * acc_ref[c] + jnp.dot(p.astype(BF16), vb, preferred_element_type=F32)
            m_ref[c] = m_new

    def diagonal(step, **kw):
        @pl.when(qi % 2 == 1)
        def _():
            step(pl.multiple_of((qi - 1) * tq, tq), wide, tq, **kw)

        @pl.when(qi % 2 == 0)
        def _():
            step(pl.multiple_of(qi * tq, tq), tq, 0, **kw)

    def off_diagonal(step):
        def body(j, carry):
            step(pl.multiple_of(j * wide, wide), wide, None)
            return carry

        lax.fori_loop(0, qi // 2, body, 0)

    @pl.when(safe)
    def _():
        diagonal(step_unshifted, first=True)
        off_diagonal(step_unshifted)
        l_ref[...] = jnp.sum(lp_ref[...], axis=-1, keepdims=True)

    @pl.when(jnp.logical_not(safe))
    def _():
        m_ref[...] = jnp.full(m_ref.shape, NEG_BIG, F32)
        l_ref[...] = jnp.zeros(l_ref.shape, F32)
        acc_ref[...] = jnp.zeros(acc_ref.shape, F32)
        off_diagonal(step_online)
        diagonal(step_online)

    lam = lam_ref[...]
    lam_full = (jnp.exp(jnp.sum(lam[0:1] * lam[1:2], axis=-1, keepdims=True))
                - jnp.exp(jnp.sum(lam[2:3] * lam[3:4], axis=-1, keepdims=True)) + lam_init)
    o = acc_ref[0] / l_ref[0] - lam_full * (acc_ref[1] / l_ref[1])
    ms = jnp.mean(o * o, axis=-1, keepdims=True)
    o_ref[0] = (o * lax.rsqrt(ms + EPS) * (sg_ref[...] * (1.0 - lam_init))).astype(o_ref.dtype)


def _scores_are_bounded(q_gain, k_gain):
    bound = (jnp.max(jnp.abs(q_gain)) * jnp.max(jnp.abs(k_gain))
             * (DA_HEAD_DIM * DA_HEAD_DIM ** -0.5 * LOG2_E * 1.02))
    return (bound <= MAX_UNSHIFTED_SCORE).astype(jnp.int32).reshape(1)


def _diff_attention(qkv, safe, lam, subln, *, batch, seq, lam_init, tq_pref=512):
    tq = _pick(seq, tq_pref)
    hw = 2 * DA_HEAD_DIM
    kern = functools.partial(_attn_kernel, tq=tq, lam_init=lam_init)
    return pl.pallas_call(
        kern,
        grid=(batch, DA_HEADS, seq // tq),
        in_specs=[
            pl.BlockSpec(memory_space=pltpu.SMEM),
            pl.BlockSpec((4, DA_HEAD_DIM), lambda b, h, i: (0, 0)),
            pl.BlockSpec((1, DA_V_DIM), lambda b, h, i: (0, 0)),
            pl.BlockSpec((1, tq, hw), lambda b, h, i: (b, i, h)),
            pl.BlockSpec((1, seq, hw), lambda b, h, i: (b, 0, DA_HEADS + h)),
            pl.BlockSpec((1, seq, DA_V_DIM), lambda b, h, i: (b, 0, 2 * DA_HEADS + h)),
        ],
        out_specs=pl.BlockSpec((1, tq, DA_V_DIM), lambda b, h, i: (b, i, h)),
        out_shape=jax.ShapeDtypeStruct((batch, seq, DA_V_WIDTH), ACT_DTYPE),
        scratch_shapes=[
            pltpu.VMEM((2, tq, 1), F32),
            pltpu.VMEM((2, tq, 1), F32),
            pltpu.VMEM((2, tq, DA_V_DIM), F32),
            pltpu.VMEM((2, tq, LANES), F32),
        ],
        compiler_params=_params(("parallel", "parallel", "arbitrary")),
        name="diff_attn",
    )(safe, lam, subln.reshape(1, DA_V_DIM), qkv, qkv, qkv)


def _split3(v):
    hi = v.astype(BF16)
    r1 = v - hi.astype(F32)
    mid = r1.astype(BF16)
    lo = (r1 - mid.astype(F32)).astype(BF16)
    return hi, mid, lo


def _ssd_prep_kernel(dt_ref, bias_ref, alog_ref, sel_ref, a2g_ref, eaw_ref, b2t_ref, *, ts):
    L = SSM_CHUNK
    dt = _softplus(dt_ref[0] + bias_ref[...])
    dta2 = dt * (-LOG2_E * jnp.exp(alog_ref[...]))
    ld2 = jnp.log(dt) * LOG2_E
    row = lax.broadcasted_iota(jnp.int32, (L, L), 0)
    col = lax.broadcasted_iota(jnp.int32, (L, L), 1)
    tril = jnp.where(row >= col, 1.0, 0.0).astype(BF16)
    for ci in range(ts // L):
        sl = slice(ci * L, (ci + 1) * L)
        hi, mid, lo = _split3(dta2[sl])
        a2 = (jnp.dot(tril, lo, preferred_element_type=F32) + jnp.dot(tril, mid, preferred_element_type=F32)
              + jnp.dot(tril, hi, preferred_element_type=F32))
        b2 = a2 - ld2[sl]
        b2t_ref[0, :, sl] = b2.T
        ea_w = jnp.concatenate([jnp.exp2(a2).astype(BF16), jnp.exp2(a2[L - 1:L, :] - b2).astype(BF16)], axis=1)
        for g in range(SSM_GROUPS):
            a2g_ref[0, g, sl, :] = a2[:, g * SSM_HPG:(g + 1) * SSM_HPG]
            eaw_ref[0, g, sl, :] = jnp.dot(ea_w, sel_ref[g], preferred_element_type=F32).astype(BF16)


def _ssd_prep(dt_raw, dt_bias, a_log, *, batch, seq, ts_pref=512):
    ts = _pick(seq, ts_pref)
    pad = LANES - SSM_HEADS
    bias = jnp.pad(dt_bias.astype(F32), (0, pad)).reshape(1, LANES)
    alog = jnp.pad(a_log.astype(F32), (0, pad)).reshape(1, LANES)
    r = jnp.arange(2 * LANES)[None, :, None]
    j = jnp.arange(LANES)[None, None, :]
    g = jnp.arange(SSM_GROUPS)[:, None, None]
    sel = (((r == g * SSM_HPG + j) & (j < SSM_HPG))
           | ((r == LANES + g * SSM_HPG + j - SSM_HPG) & (j >= SSM_HPG) & (j < 2 * SSM_HPG))).astype(BF16)
    col = pl.BlockSpec((1, ts, LANES), lambda b, c: (b, c, 0))
    rowl = pl.BlockSpec((1, LANES, ts), lambda b, c: (b, 0, c))
    vec = pl.BlockSpec((1, LANES), lambda b, c: (0, 0))
    return pl.pallas_call(
        functools.partial(_ssd_prep_kernel, ts=ts),
        grid=(batch, seq // ts),
        in_specs=[col, vec, vec, pl.BlockSpec((SSM_GROUPS, 2 * LANES, LANES), lambda b, c: (0, 0, 0))],
        out_specs=[pl.BlockSpec((1, SSM_GROUPS, ts, SSM_HPG), lambda b, c: (b, 0, c, 0)),
                   pl.BlockSpec((1, SSM_GROUPS, ts, LANES), lambda b, c: (b, 0, c, 0)),
                   rowl],
        out_shape=[jax.ShapeDtypeStruct((batch, SSM_GROUPS, seq, SSM_HPG), F32),
                   jax.ShapeDtypeStruct((batch, SSM_GROUPS, seq, LANES), BF16),
                   jax.ShapeDtypeStruct((batch, LANES, seq), F32)],
        compiler_params=_params(("parallel", "parallel")),
        name="ssd_prep",
    )(dt_raw.reshape(batch, seq, LANES), bias, alog, sel)


def _ssd_kernel(x_ref, z_ref, b_ref, c_ref, cwx_ref, cwb_ref, cwc_ref, cbx_ref, cbb_ref, cbc_ref,
                a2g_ref, eaw_ref, b2t_ref, shift_ref, expand_ref, d_ref, ng_ref, o_ref, tail_ref, st_ref, *, ts):
    L = SSM_CHUNK
    gw, n = SSM_GROUP_WIDTH, SSM_D_STATE
    cidx = pl.program_id(2)

    slot = cidx % 2

    @pl.when(cidx == 0)
    def _():
        tail_ref[0] = jnp.zeros(tail_ref.shape[1:], tail_ref.dtype)
        st_ref[...] = jnp.zeros(st_ref.shape, F32)

    def rows(lo, hi):
        return jnp.concatenate([x_ref[0, lo:hi, :], b_ref[0, lo:hi, :], c_ref[0, lo:hi, :]], axis=1)

    row = lax.broadcasted_iota(jnp.int32, (L, L), 0)
    col = lax.broadcasted_iota(jnp.int32, (L, L), 1)
    causal = row >= col
    low_half = lax.broadcasted_iota(jnp.int32, (1, LANES), 1) < SSM_HEAD_DIM
    lane = lax.broadcasted_iota(jnp.int32, (L, LANES), 1)
    keep_lo = jnp.where(lane < SSM_HEAD_DIM, 1.0, 0.0).astype(BF16)
    keep_hi = jnp.where(lane < SSM_HEAD_DIM, 0.0, 1.0).astype(BF16)

    n_shift = SSM_CONV - 1
    conv_w = jnp.concatenate([cwx_ref[...], cwb_ref[...], cwc_ref[...]], axis=1)
    conv_b = jnp.concatenate([cbx_ref[...], cbb_ref[...], cbc_ref[...]], axis=1)

    st = st_ref[...]
    for ci in range(ts // L):
        r0 = ci * L
        win = jnp.concatenate([tail_ref[slot], rows(0, L)], axis=0) if ci == 0 else rows(r0 - L, r0 + L)
        lagged = jnp.dot(shift_ref[...], win, preferred_element_type=F32)
        acc = conv_b + conv_w[n_shift:n_shift + 1, :] * win[L:2 * L, :].astype(F32)
        for k in range(n_shift):
            acc = acc + conv_w[k:k + 1, :] * lagged[k * L:(k + 1) * L, :]
        act = acc * _sigmoid(acc)
        xc, bc, cc = act[:, 0:gw], act[:, gw:gw + n], act[:, gw + n:gw + 2 * n]
        cb = lax.dot_general(cc.astype(BF16), bc.astype(BF16), (((1,), (1,)), ((), ())),
                             preferred_element_type=F32)
        cbb = cb.astype(BF16)
        btb = bc.T.astype(BF16)
        a2c = a2g_ref[0, 0, r0:r0 + L, :]
        b2r = b2t_ref[0, :, r0:r0 + L]
        dlast = jnp.exp2(a2c[L - 1:L, :])
        spread = jnp.dot(eaw_ref[0, 0, r0:r0 + L, :], expand_ref[...], preferred_element_type=F32)
        ea_x, w_x = spread[:, 0:gw], spread[:, gw:2 * gw]
        xcb = xc.astype(BF16)
        y_off = jnp.dot(cc.astype(BF16), st.astype(BF16), preferred_element_type=F32)
        xw = (xc * w_x).astype(BF16)
        s_new = jnp.dot(btb, xw, preferred_element_type=F32)
        ys, ds = [], []
        for j in range(SSM_HPG // 2):
            xpair = xcb[:, j * LANES:(j + 1) * LANES]
            rhs = jnp.concatenate([xpair * keep_lo, xpair * keep_hi], axis=0)
            ms2 = []
            for h in (2 * j, 2 * j + 1):
                seg = a2c[:, h:h + 1] - b2r[h:h + 1, :]
                ms2.append(cbb * jnp.exp2(jnp.where(causal, seg, -jnp.inf)).astype(BF16))
            ys.append(jnp.dot(jnp.concatenate(ms2, axis=1), rhs, preferred_element_type=F32))
            ds.append(jnp.where(low_half, dlast[:, 2 * j:2 * j + 1], dlast[:, 2 * j + 1:2 * j + 2]))
        st = st * jnp.concatenate(ds, axis=1) + s_new
        y = jnp.concatenate(ys, axis=1) + y_off * ea_x + d_ref[...] * xc
        zz = z_ref[0, r0:r0 + L, :].astype(F32)
        y = y * (zz * _sigmoid(zz))
        ms = jnp.mean(y * y, axis=-1, keepdims=True)
        o_ref[0, r0:r0 + L, :] = (y * lax.rsqrt(ms + EPS) * ng_ref[...]).astype(o_ref.dtype)

    st_ref[...] = st
    tail_ref[1 - slot] = rows(ts - L, ts)


def _ssd(zx, a2_g, eaw_g, b2_t, conv_w, conv_b, d_skip, norm_gain, *, batch, seq, ts_pref=2048):
    assert zx.dtype == BF16, "the shift-matrix conv is exact only for bf16 inputs"
    ts = _pick(seq, ts_pref)
    gw, n, G = SSM_GROUP_WIDTH, SSM_D_STATE, SSM_GROUPS
    L, n_shift = SSM_CHUNK, SSM_CONV - 1
    srow = jnp.arange(n_shift * L)[:, None]
    shift = (jnp.arange(2 * L)[None, :] == (srow % L) + (L - n_shift) + srow // L).astype(BF16)
    er = jnp.arange(LANES)[:, None]
    ec = jnp.arange(2 * gw)[None, :]
    expand = (er == (ec % gw) // SSM_HEAD_DIM + SSM_HPG * (ec // gw)).astype(BF16)
    x_blk0 = SSM_D_INNER // gw
    b_blk0 = 2 * SSM_D_INNER // n
    c_blk0 = b_blk0 + SSM_BC_WIDTH // n
    cw = conv_w.astype(F32)
    cbias = conv_b.astype(F32).reshape(1, -1)
    d_exp = jnp.repeat(d_skip.astype(F32), SSM_HEAD_DIM).reshape(1, SSM_D_INNER)
    ng = norm_gain.astype(F32).reshape(1, SSM_D_INNER)
    cwb0 = SSM_D_INNER // n
    cwc0 = cwb0 + SSM_BC_WIDTH // n
    in_specs = [
        pl.BlockSpec((1, ts, gw), lambda b, g, c: (b, c, x_blk0 + g)),
        pl.BlockSpec((1, ts, gw), lambda b, g, c: (b, c, g)),
        pl.BlockSpec((1, ts, n), lambda b, g, c: (b, c, b_blk0 + g)),
        pl.BlockSpec((1, ts, n), lambda b, g, c: (b, c, c_blk0 + g)),
        pl.BlockSpec((SSM_CONV, gw), lambda b, g, c: (0, g)),
        pl.BlockSpec((SSM_CONV, n), lambda b, g, c: (0, cwb0 + g)),
        pl.BlockSpec((SSM_CONV, n), lambda b, g, c: (0, cwc0 + g)),
        pl.BlockSpec((1, gw), lambda b, g, c: (0, g)),
        pl.BlockSpec((1, n), lambda b, g, c: (0, cwb0 + g)),
        pl.BlockSpec((1, n), lambda b, g, c: (0, cwc0 + g)),
        pl.BlockSpec((1, 1, ts, SSM_HPG), lambda b, g, c: (b, g, c, 0)),
        pl.BlockSpec((1, 1, ts, LANES), lambda b, g, c: (b, g, c, 0)),
        pl.BlockSpec((1, SSM_HPG, ts), lambda b, g, c: (b, g, c)),
        pl.BlockSpec((n_shift * L, 2 * L), lambda b, g, c: (0, 0)),
        pl.BlockSpec((LANES, 2 * gw), lambda b, g, c: (0, 0)),
        pl.BlockSpec((1, gw), lambda b, g, c: (0, g)),
        pl.BlockSpec((1, gw), lambda b, g, c: (0, g)),
    ]
    return pl.pallas_call(
        functools.partial(_ssd_kernel, ts=ts),
        grid=(batch, G, seq // ts),
        in_specs=in_specs,
        out_specs=pl.BlockSpec((1, ts, gw), lambda b, g, c: (b, c, g)),
        out_shape=jax.ShapeDtypeStruct((batch, seq, SSM_D_INNER), ACT_DTYPE),
        scratch_shapes=[
            pltpu.VMEM((2, SSM_CHUNK, gw + 2 * n), BF16),
            pltpu.VMEM((n, gw), F32),
        ],
        compiler_params=_params(("parallel", "parallel", "arbitrary")),
        name="ssd_scan",
    )(zx, zx, zx, zx, cw, cw, cw, cbias, cbias, cbias, a2_g, eaw_g, b2_t, shift, expand, d_exp, ng)


def _rope_tables(seq):
    inv = ROPE_THETA ** (-jnp.arange(0, ROPE_DIM, 2, dtype=F32) / ROPE_DIM)
    ang = jnp.arange(seq, dtype=F32)[:, None] * inv[None, :]
    cos, sin = jnp.cos(ang), jnp.sin(ang)
    cosf = jnp.concatenate([cos, cos, jnp.ones((seq, LANES - ROPE_DIM), F32)], axis=1)
    sins = jnp.concatenate([-sin, sin, jnp.zeros((seq, LANES - ROPE_DIM), F32)], axis=1)
    return cosf, sins


def _attn_layer(h, norm_g, w_qkv, layer, q_gain, k_gain, lam, subln, w_o, lam_init, tables, *, batch, seq):
    q_gain, k_gain = q_gain.astype(F32), k_gain.astype(F32)
    qkv = _norm_matmul(h, norm_g, w_qkv, layer, mode="qkv", out_dtype=ACT_DTYPE,
                       qkv_extra=(q_gain, k_gain) + tables, seq=seq)
    safe = _scores_are_bounded(q_gain, k_gain)
    o = _diff_attention(qkv.reshape(batch, seq, -1), safe, lam.astype(F32), subln.astype(F32),
                        batch=batch, seq=seq, lam_init=lam_init)
    return _matmul_residual(o.reshape(batch * seq, DA_V_WIDTH), w_o, layer, h)


def _ssm_layer(h, norm_g, w_in, w_dt, layer, conv_w, conv_b, dt_bias, a_log, d_skip, norm_gain, w_out,
               *, batch, seq):
    zx, dt_raw = _norm_matmul(h, norm_g, w_in, layer, mode="plain_side", out_dtype=ACT_DTYPE, n_out=SSM_ZX_WIDTH,
                              w_side=w_dt)
    a2_g, eaw_g, b2_t = _ssd_prep(dt_raw, dt_bias, a_log, batch=batch, seq=seq)
    y = _ssd(zx.reshape(batch, seq, SSM_ZX_WIDTH), a2_g, eaw_g, b2_t, conv_w, conv_b, d_skip, norm_gain,
             batch=batch, seq=seq)
    return _matmul_residual(y.reshape(batch * seq, SSM_D_INNER), w_out, layer, h)


def _mlp_layer(h, norm_g, w1, w2, layer):
    a = _norm_matmul(h, norm_g, w1, layer, mode="relu2", out_dtype=ACT_DTYPE)
    return _matmul_residual(a, w2, layer, h)


def kernel(x, mixer_norm, mlp_norm, attn_w_qkv, attn_q_norm, attn_k_norm, attn_lambda, attn_subln, attn_w_o, ssm_w_in, ssm_conv_w, ssm_conv_b, ssm_dt_bias, ssm_a_log, ssm_d, ssm_norm, ssm_w_out, mlp_w1, mlp_w2):
    batch, seq, d_model = x.shape
    depth = mixer_norm.shape[0]
    h = x.reshape(batch * seq, d_model).astype(F32)
    tables = _rope_tables(seq)
    w_qkv = attn_w_qkv.astype(BF16)
    w_o = attn_w_o.astype(BF16)
    w_in = ssm_w_in.astype(BF16)
    w_dt = jnp.pad(ssm_w_in[..., SSM_ZX_WIDTH:], ((0, 0), (0, 0), (0, LANES - SSM_HEADS))).astype(BF16)
    w_out = ssm_w_out.astype(BF16)
    w1 = mlp_w1.astype(BF16)
    w2 = mlp_w2.astype(BF16)
    for i in range(depth):
        j = i // 2
        if i % 2 == 0:
            h = _attn_layer(h, mixer_norm[i].astype(F32), w_qkv, j, attn_q_norm[j], attn_k_norm[j],
                            attn_lambda[j], attn_subln[j], w_o, _lambda_init(i), tables,
                            batch=batch, seq=seq)
        else:
            h = _ssm_layer(h, mixer_norm[i].astype(F32), w_in, w_dt, j, ssm_conv_w[j], ssm_conv_b[j],
                           ssm_dt_bias[j], ssm_a_log[j], ssm_d[j], ssm_norm[j], w_out,
                           batch=batch, seq=seq)
        h = _mlp_layer(h, mlp_norm[i].astype(F32), w1, w2, i)
    return h.reshape(batch, seq, d_model).astype(x.dtype)
```
